```python
import jax, jax.numpy as jnp
from jax import lax
import numpy as np

D_MODEL = 2048
BATCH = 16
SEQ = 2048
DEPTH = 1

N_HEADS = 16
QK_NOPE_DIM = 128
QK_ROPE_DIM = 64
QK_HEAD_DIM = QK_NOPE_DIM + QK_ROPE_DIM
V_HEAD_DIM = 128
Q_LORA_RANK = 768
KV_LORA_RANK = 512
ROPE_THETA = 10000.0
Q_BLOCK = 128
ATTN_SCALE = QK_HEAD_DIM ** -0.5
CONV_CH = D_MODEL
CONV_WIDTH = 31
N_EXPERTS = 32
TOP_K = 4
D_FF = D_MODEL
SWIGLU_LIMIT = 7.0
SWIGLU_ALPHA = 1.702
N_BRANCHES = 2
NORM_EPS = 1e-6
IN_COLS = Q_LORA_RANK + KV_LORA_RANK + QK_ROPE_DIM + 2 * CONV_CH + N_BRANCHES * D_MODEL

kernel_name = "hybrid_mla_conformer_moe_block"


def rmsnorm(x, g):
    xf = x.astype(jnp.float32)
    y = xf * lax.rsqrt(jnp.mean(xf * xf, axis=-1, keepdims=True) + NORM_EPS)
    return (y * g.astype(jnp.float32)).astype(x.dtype)


def layernorm(x, g, b):
    xf = x.astype(jnp.float32)
    mu = jnp.mean(xf, axis=-1, keepdims=True)
    var = jnp.mean(jnp.square(xf - mu), axis=-1, keepdims=True)
    y = (xf - mu) * lax.rsqrt(var + NORM_EPS)
    return (y * g.astype(jnp.float32) + b.astype(jnp.float32)).astype(x.dtype)


def rope_cos_sin(positions):
    inv_freq = ROPE_THETA ** (-jnp.arange(0, QK_ROPE_DIM, 2, dtype=jnp.float32) / QK_ROPE_DIM)
    ang = positions.astype(jnp.float32)[..., None] * inv_freq
    return jnp.cos(ang), jnp.sin(ang)


def apply_rope(x, cos, sin):
    half = x.shape[-1] // 2
    x1, x2 = x[..., :half], x[..., half:]
    cos = cos.astype(x.dtype)
    sin = sin.astype(x.dtype)
    return jnp.concatenate([x1 * cos - x2 * sin, x2 * cos + x1 * sin], axis=-1)


def mla_branch(c_q, c_kv, k_rope, cos, sin, g_q, w_q_up, g_kv, w_kv_up, w_attn_o):
    B, S, _ = c_q.shape
    q = (rmsnorm(c_q, g_q) @ w_q_up).reshape(B, S, N_HEADS, QK_HEAD_DIM)
    q_nope = q[..., :QK_NOPE_DIM]
    q_rope = apply_rope(q[..., QK_NOPE_DIM:], cos[:, :, None, :], sin[:, :, None, :])
    kv = (rmsnorm(c_kv, g_kv) @ w_kv_up).reshape(B, S, N_HEADS, QK_NOPE_DIM + V_HEAD_DIM)
    k_nope = kv[..., :QK_NOPE_DIM]
    v = kv[..., QK_NOPE_DIM:]
    k_rope = apply_rope(k_rope, cos, sin)

    nb = S // Q_BLOCK
    qn_blocks = q_nope.reshape(B, nb, Q_BLOCK, N_HEADS, QK_NOPE_DIM).transpose(1, 0, 2, 3, 4)
    qr_blocks = q_rope.reshape(B, nb, Q_BLOCK, N_HEADS, QK_ROPE_DIM).transpose(1, 0, 2, 3, 4)
    starts = jnp.arange(nb, dtype=jnp.int32) * Q_BLOCK
    key_idx = jnp.arange(S, dtype=jnp.int32)

    def attend(args):
        qn_b, qr_b, start = args
        s = (jnp.einsum('bqhd,bkhd->bhqk', qn_b, k_nope)
             + jnp.einsum('bqhr,bkr->bhqk', qr_b, k_rope))
        s = s.astype(jnp.float32) * ATTN_SCALE
        q_idx = start + jnp.arange(Q_BLOCK, dtype=jnp.int32)
        causal = key_idx[None, :] <= q_idx[:, None]
        s = jnp.where(causal[None, None], s, -jnp.inf)
        p = jax.nn.softmax(s, axis=-1).astype(v.dtype)
        return jnp.einsum('bhqk,bkhd->bqhd', p, v)

    o = lax.map(attend, (qn_blocks, qr_blocks, starts))
    o = o.transpose(1, 0, 2, 3, 4).reshape(B, S, N_HEADS * V_HEAD_DIM)
    return o @ w_attn_o


def conv_branch(u, w_dw, b_dw, g_ln, b_ln, w_conv_out):
    a, b = u[..., :CONV_CH], u[..., CONV_CH:]
    glu = a * jax.nn.sigmoid(b)
    c = lax.conv_general_dilated(
        glu, w_dw[:, None, :], window_strides=(1,), padding=[(CONV_WIDTH - 1, 0)],
        dimension_numbers=('NWC', 'WIO', 'NWC'), feature_group_count=CONV_CH) + b_dw
    c = jax.nn.silu(layernorm(c, g_ln, b_ln))
    return c @ w_conv_out


def moe_ffn(t, w_router, b_router, w_gate_up, b_gate_up, w_down, b_down):
    B, S, D = t.shape
    tf = t.reshape(B * S, D)
    logits = (tf @ w_router + b_router).astype(jnp.float32)
    top_vals, top_idx = lax.top_k(logits, TOP_K)
    top_w = jax.nn.softmax(top_vals, axis=-1)
    gates = jnp.sum(jax.nn.one_hot(top_idx, N_EXPERTS, dtype=jnp.float32) * top_w[..., None], axis=1)
    gates = gates.astype(t.dtype)
    out = jnp.zeros_like(tf)
    for e in range(N_EXPERTS):
        gu = tf @ w_gate_up[e] + b_gate_up[e]
        g = jnp.minimum(gu[:, :D_FF], SWIGLU_LIMIT)
        up = jnp.clip(gu[:, D_FF:], -SWIGLU_LIMIT, SWIGLU_LIMIT)
        hdn = (up + 1.0) * (g * jax.nn.sigmoid(SWIGLU_ALPHA * g))
        out = out + gates[:, e:e + 1] * (hdn @ w_down[e] + b_down[e])
    return out.reshape(B, S, D)


def setup_inputs(seed: int = 0) -> dict:
    key = jax.random.key(seed)
    ks = jax.random.split(key, 24)
    L = DEPTH

    def nrm(k, shape, fan_in):
        return jax.random.normal(k, shape, jnp.float32) * fan_in ** -0.5

    def gain(k, shape):
        return 1.0 + 0.02 * jax.random.normal(k, shape, jnp.float32)

    def small(k, shape, s=0.02):
        return s * jax.random.normal(k, shape, jnp.float32)

    x = jax.random.normal(ks[0], (BATCH, SEQ, D_MODEL), jnp.float32)
    offsets = jax.random.randint(ks[1], (BATCH, 1), 0, 4096, dtype=jnp.int32)
    positions = offsets + jnp.arange(SEQ, dtype=jnp.int32)[None, :]
    return {
        "x": x,
        "positions": positions,
        "g_mix": gain(ks[2], (L, D_MODEL)),
        "w_in": nrm(ks[3], (L, D_MODEL, IN_COLS), D_MODEL),
        "g_q": gain(ks[4], (L, Q_LORA_RANK)),
        "w_q_up": nrm(ks[5], (L, Q_LORA_RANK, N_HEADS * QK_HEAD_DIM), Q_LORA_RANK),
        "g_kv": gain(ks[6], (L, KV_LORA_RANK)),
        "w_kv_up": nrm(ks[7], (L, KV_LORA_RANK, N_HEADS * (QK_NOPE_DIM + V_HEAD_DIM)), KV_LORA_RANK),
        "w_attn_o": nrm(ks[8], (L, N_HEADS * V_HEAD_DIM, D_MODEL), N_HEADS * V_HEAD_DIM),
        "w_dw": nrm(ks[9], (L, CONV_WIDTH, CONV_CH), CONV_WIDTH),
        "b_dw": small(ks[10], (L, CONV_CH)),
        "g_conv_ln": gain(ks[11], (L, CONV_CH)),
        "b_conv_ln": small(ks[12], (L, CONV_CH)),
        "w_conv_out": nrm(ks[13], (L, CONV_CH, D_MODEL), CONV_CH),
        "w_out": nrm(ks[14], (L, D_MODEL, D_MODEL), D_MODEL),
        "g_ffn": gain(ks[15], (L, D_MODEL)),
        "w_router": nrm(ks[16], (L, D_MODEL, N_EXPERTS), D_MODEL),
        "b_router": small(ks[17], (L, N_EXPERTS), 0.01),
        "w_gate_up": nrm(ks[18], (L, N_EXPERTS, D_MODEL, 2 * D_FF), D_MODEL),
        "b_gate_up": small(ks[19], (L, N_EXPERTS, 2 * D_FF)),
        "w_down": nrm(ks[20], (L, N_EXPERTS, D_FF, D_MODEL), D_FF),
        "b_down": small(ks[21], (L, N_EXPERTS, D_MODEL)),
        "g_final": gain(ks[22], (D_MODEL,)),
    }


def reference(x, positions, g_mix, w_in, g_q, w_q_up, g_kv, w_kv_up, w_attn_o,
              w_dw, b_dw, g_conv_ln, b_conv_ln, w_conv_out, w_out, g_ffn,
              w_router, b_router, w_gate_up, b_gate_up, w_down, b_down, g_final):
    cos, sin = rope_cos_sin(positions)
    o_q = Q_LORA_RANK
    o_kv = o_q + KV_LORA_RANK
    o_kr = o_kv + QK_ROPE_DIM
    o_conv = o_kr + 2 * CONV_CH
    o_ga = o_conv + D_MODEL
    for l in range(DEPTH):
        h = rmsnorm(x, g_mix[l])
        p = h @ w_in[l]
        y_a = mla_branch(p[..., :o_q], p[..., o_q:o_kv], p[..., o_kv:o_kr], cos, sin,
                         g_q[l], w_q_up[l], g_kv[l], w_kv_up[l], w_attn_o[l])
        y_b = conv_branch(p[..., o_kr:o_conv], w_dw[l], b_dw[l], g_conv_ln[l], b_conv_ln[l],
                          w_conv_out[l])
        merged = jax.nn.sigmoid(p[..., o_conv:o_ga]) * y_a + jax.nn.sigmoid(p[..., o_ga:]) * y_b
        x = x + merged @ w_out[l]
        x = x + moe_ffn(rmsnorm(x, g_ffn[l]), w_router[l], b_router[l],
                        w_gate_up[l], b_gate_up[l], w_down[l], b_down[l])
    return rmsnorm(x, g_final)
```

```python
import functools

import jax
import jax.numpy as jnp
from jax import lax
from jax.experimental import pallas as pl
from jax.experimental.pallas import tpu as pltpu

N_HEADS = 16
QK_NOPE_DIM = 128
QK_ROPE_DIM = 64
V_HEAD_DIM = 128
Q_LORA_RANK = 768
KV_LORA_RANK = 512
ROPE_THETA = 10000.0
ATTN_SCALE = (QK_NOPE_DIM + QK_ROPE_DIM) ** -0.5
CONV_WIDTH = 31
N_EXPERTS = 32
TOP_K = 4
SWIGLU_LIMIT = 7.0
SWIGLU_ALPHA = 1.702
NORM_EPS = 1e-6

LANES = 128
SUBLANES = 8
HEAD_PAD = 2 * LANES
VMEM_LIMIT = 56 * 1024 * 1024

F32 = jnp.float32
BF16 = jnp.bfloat16


def _cparams(*sem):
    return pltpu.CompilerParams(dimension_semantics=sem, vmem_limit_bytes=VMEM_LIMIT)


def _rms(x, g):
    ms = jnp.mean(x * x, axis=-1, keepdims=True)
    return x * lax.rsqrt(ms + NORM_EPS) * g


def _rope_tables_kernel(pos_ref, invf_ref, cos_ref, sin_ref):
    ang = pos_ref[...].astype(F32) * invf_ref[...]
    lane = lax.broadcasted_iota(jnp.int32, ang.shape, 1)
    c = jnp.cos(ang)
    s = jnp.sin(ang)
    cos_ref[...] = jnp.where(lane < QK_ROPE_DIM, c, 0.0)
    sin_ref[...] = jnp.where(lane < QK_ROPE_DIM // 2, -s, jnp.where(lane < QK_ROPE_DIM, s, 0.0))


def _rope_tables(pos_b, invf, tm):
    n = pos_b.shape[0]
    row = pl.BlockSpec((tm, LANES), lambda i: (i, 0))
    return pl.pallas_call(
        _rope_tables_kernel, grid=(n // tm,),
        in_specs=[row, pl.BlockSpec((1, LANES), lambda i: (0, 0))],
        out_specs=[row, row],
        out_shape=[jax.ShapeDtypeStruct((n, LANES), F32)] * 2,
        compiler_params=_cparams("parallel"), name="rope_tables")(pos_b, invf)


def _rope(r, cos, sin):
    half = QK_ROPE_DIM // 2
    lane = lax.broadcasted_iota(jnp.int32, r.shape, 1)
    rot = jnp.where(lane < half, pltpu.roll(r, LANES - half, 1), pltpu.roll(r, half, 1))
    return r * cos + rot * sin


def _in_proj_kernel(x_ref, g_ref, w_ref, o_ref, h_ref):
    @pl.when(pl.program_id(1) == 0)
    def _():
        h_ref[...] = _rms(x_ref[...], g_ref[...]).astype(h_ref.dtype)

    o_ref[...] = jnp.dot(h_ref[...], w_ref[...], preferred_element_type=F32).astype(o_ref.dtype)


def _in_proj(x2d, g, w, tm, tn):
    n, d = x2d.shape
    cols = w.shape[1]
    return pl.pallas_call(
        _in_proj_kernel, grid=(n // tm, cols // tn),
        in_specs=[pl.BlockSpec((tm, d), lambda i, j: (i, 0)),
                  pl.BlockSpec((1, d), lambda i, j: (0, 0)),
                  pl.BlockSpec((d, tn), lambda i, j: (0, j))],
        out_specs=pl.BlockSpec((tm, tn), lambda i, j: (i, j)),
        out_shape=jax.ShapeDtypeStruct((n, cols), BF16),
        scratch_shapes=[pltpu.VMEM((tm, d), BF16)],
        compiler_params=_cparams("parallel", "arbitrary"), name="in_proj")(x2d, g, w)


def _q_up_kernel(c_ref, g_ref, w_ref, cos_ref, sin_ref, o_ref, cn_ref, *, hb):
    @pl.when(pl.program_id(1) == 0)
    def _():
        cn_ref[...] = _rms(c_ref[...].astype(F32), g_ref[...]).astype(cn_ref.dtype)

    q = jnp.dot(cn_ref[...], w_ref[...], preferred_element_type=F32)
    cos = cos_ref[...]
    sin = sin_ref[...]
    for h in range(hb):
        b0 = h * HEAD_PAD
        o_ref[:, b0:b0 + LANES] = (q[:, b0:b0 + LANES] * ATTN_SCALE).astype(o_ref.dtype)
        o_ref[:, b0 + LANES:b0 + HEAD_PAD] = (
            _rope(q[:, b0 + LANES:b0 + HEAD_PAD], cos, sin) * ATTN_SCALE).astype(o_ref.dtype)


def _q_up(p, g, w, cos_t, sin_t, tm, hb):
    n = p.shape[0]
    r = Q_LORA_RANK
    row = pl.BlockSpec((tm, LANES), lambda i, j: (i, 0))
    return pl.pallas_call(
        functools.partial(_q_up_kernel, hb=hb), grid=(n // tm, N_HEADS // hb),
        in_specs=[pl.BlockSpec((tm, r), lambda i, j: (i, 1)),
                  pl.BlockSpec((1, r), lambda i, j: (0, 0)),
                  pl.BlockSpec((r, hb * HEAD_PAD), lambda i, j: (0, j)),
                  row, row],
        out_specs=pl.BlockSpec((tm, hb * HEAD_PAD), lambda i, j: (i, j)),
        out_shape=jax.ShapeDtypeStruct((n, N_HEADS * HEAD_PAD), BF16),
        scratch_shapes=[pltpu.VMEM((tm, r), BF16)],
        compiler_params=_cparams("parallel", "arbitrary"), name="q_up")(p, g, w, cos_t, sin_t)


def _kv_up_kernel(c_ref, g_ref, wk_ref, wv_ref, cos_ref, sin_ref, k_ref, v_ref, cn_ref, kr_ref, *, hb):
    @pl.when(pl.program_id(1) == 0)
    def _():
        cn_ref[...] = _rms(c_ref[:, :KV_LORA_RANK].astype(F32), g_ref[...]).astype(cn_ref.dtype)
        kr = c_ref[:, KV_LORA_RANK:KV_LORA_RANK + LANES].astype(F32)
        kr_ref[...] = _rope(kr, cos_ref[...], sin_ref[...]).astype(kr_ref.dtype)

    cn = cn_ref[...]
    kn = jnp.dot(cn, wk_ref[...], preferred_element_type=F32)
    for h in range(hb):
        k_ref[:, h * HEAD_PAD:h * HEAD_PAD + LANES] = kn[:, h * LANES:(h + 1) * LANES].astype(k_ref.dtype)
        k_ref[:, h * HEAD_PAD + LANES:(h + 1) * HEAD_PAD] = kr_ref[...]
    v_ref[...] = jnp.dot(cn, wv_ref[...], preferred_element_type=F32).astype(v_ref.dtype)


def _kv_up(p, g, wk, wv, cos_t, sin_t, tm, hb):
    n = p.shape[0]
    r = KV_LORA_RANK
    cw = Q_LORA_RANK
    row = pl.BlockSpec((tm, LANES), lambda i, j: (i, 0))
    return pl.pallas_call(
        functools.partial(_kv_up_kernel, hb=hb), grid=(n // tm, N_HEADS // hb),
        in_specs=[pl.BlockSpec((tm, cw), lambda i, j: (i, 0)),
                  pl.BlockSpec((1, r), lambda i, j: (0, 0)),
                  pl.BlockSpec((r, hb * LANES), lambda i, j: (0, j)),
                  pl.BlockSpec((r, hb * LANES), lambda i, j: (0, j)),
                  row, row],
        out_specs=[pl.BlockSpec((tm, hb * HEAD_PAD), lambda i, j: (i, j)),
                   pl.BlockSpec((tm, hb * LANES), lambda i, j: (i, j))],
        out_shape=[jax.ShapeDtypeStruct((n, N_HEADS * HEAD_PAD), BF16),
                   jax.ShapeDtypeStruct((n, N_HEADS * V_HEAD_DIM), BF16)],
        scratch_shapes=[pltpu.VMEM((tm, r), BF16), pltpu.VMEM((tm, LANES), BF16)],
        compiler_params=_cparams("parallel", "arbitrary"), name="kv_up")(p, g, wk, wv, cos_t, sin_t)


def _attn_kernel(q_ref, k_ref, v_ref, o_ref, *, tq):
    s_len = q_ref.shape[0]
    dn = (((1,), (1,)), ((), ()))
    row = lax.broadcasted_iota(jnp.int32, (tq, tq), 0)
    col = lax.broadcasted_iota(jnp.int32, (tq, tq), 1)
    for i in range(s_len // tq):
        lo = i * tq
        q = q_ref[lo:lo + tq, :]
        sd = lax.dot_general(q, k_ref[lo:lo + tq, :], dn, preferred_element_type=F32)
        sd = jnp.where(col <= row, sd, -jnp.inf)
        m = jnp.max(sd, axis=-1, keepdims=True)
        if i > 0:
            sp = lax.dot_general(q, k_ref[0:lo, :], dn, preferred_element_type=F32)
            m = jnp.maximum(m, jnp.max(sp, axis=-1, keepdims=True))
            pp = jnp.exp(sp - m)
            l = jnp.sum(pp, axis=-1, keepdims=True)
            acc = jnp.dot(pp.astype(v_ref.dtype), v_ref[0:lo, :], preferred_element_type=F32)
        pd = jnp.exp(sd - m)
        ld = jnp.sum(pd, axis=-1, keepdims=True)
        accd = jnp.dot(pd.astype(v_ref.dtype), v_ref[lo:lo + tq, :], preferred_element_type=F32)
        if i > 0:
            ld = ld + l
            accd = accd + acc
        o_ref[lo:lo + tq, :] = (accd / ld).astype(o_ref.dtype)


def _attention(q, k, v, batch, seq, tq):
    n = q.shape[0]
    return pl.pallas_call(
        functools.partial(_attn_kernel, tq=tq), grid=(batch, N_HEADS),
        in_specs=[pl.BlockSpec((seq, HEAD_PAD), lambda b, h: (b, h)),
                  pl.BlockSpec((seq, HEAD_PAD), lambda b, h: (b, h)),
                  pl.BlockSpec((seq, V_HEAD_DIM), lambda b, h: (b, h))],
        out_specs=pl.BlockSpec((seq, V_HEAD_DIM), lambda b, h: (b, h)),
        out_shape=jax.ShapeDtypeStruct((n, N_HEADS * V_HEAD_DIM), BF16),
        compiler_params=_cparams("parallel", "parallel"), name="attention")(q, k, v)


CONV_HALO = 32
CONV_CW = 512
CONV_RC = 32


def _conv_kernel(a_ref, b_ref, w_ref, bdw_ref, g_ref, bln_ref, o_ref, gbuf, cbuf, *, ts):
    si = pl.program_id(1)
    ch = a_ref.shape[1]
    ncc = ch // CONV_CW

    @pl.when(si == 0)
    def _():
        gbuf[:, 0:CONV_HALO, :] = jnp.zeros((ncc, CONV_HALO, CONV_CW), F32)

    @pl.when(si > 0)
    def _():
        gbuf[:, 0:CONV_HALO, :] = gbuf[:, ts:ts + CONV_HALO, :]

    for cc in range(ncc):
        a = a_ref[:, cc * CONV_CW:(cc + 1) * CONV_CW].astype(F32)
        b = b_ref[:, cc * CONV_CW:(cc + 1) * CONV_CW].astype(F32)
        gbuf[cc, CONV_HALO:CONV_HALO + ts, :] = a * jax.nn.sigmoid(b)

    def chunk(cc, carry):
        for r in range(ts // CONV_RC):
            acc = jnp.zeros((CONV_RC, CONV_CW), F32)
            for kk in range(CONV_WIDTH):
                off = r * CONV_RC + CONV_HALO - (CONV_WIDTH - 1) + kk
                acc = acc + w_ref[cc, kk:kk + 1, :] * gbuf[cc, off:off + CONV_RC, :]
            cbuf[cc, r * CONV_RC:(r + 1) * CONV_RC, :] = acc
        return carry

    lax.fori_loop(0, ncc, chunk, 0)

    c = [cbuf[cc] + bdw_ref[:, cc * CONV_CW:(cc + 1) * CONV_CW] for cc in range(ncc)]
    mu = sum(jnp.sum(x, axis=-1, keepdims=True) for x in c) / ch
    var = sum(jnp.sum(jnp.square(x - mu), axis=-1, keepdims=True) for x in c) / ch
    rstd = lax.rsqrt(var + NORM_EPS)
    for cc in range(ncc):
        sl = slice(cc * CONV_CW, (cc + 1) * CONV_CW)
        y = (c[cc] - mu) * rstd * g_ref[:, sl] + bln_ref[:, sl]
        o_ref[:, sl] = (y * jax.nn.sigmoid(y)).astype(o_ref.dtype)


def _conv_module(p, w4, b_dw, g_ln, b_ln, batch, seq, ch, ts):
    n = p.shape[0]
    nseq = seq // ts
    ncc = ch // CONV_CW
    vec = pl.BlockSpec((1, ch), lambda b, s: (0, 0))
    return pl.pallas_call(
        functools.partial(_conv_kernel, ts=ts), grid=(batch, nseq),
        in_specs=[pl.BlockSpec((ts, ch), lambda b, s: (b * nseq + s, 1)),
                  pl.BlockSpec((ts, ch), lambda b, s: (b * nseq + s, 2)),
                  pl.BlockSpec((ncc, CONV_HALO, CONV_CW), lambda b, s: (0, 0, 0)),
                  vec, vec, vec],
        out_specs=pl.BlockSpec((ts, ch), lambda b, s: (b * nseq + s, 0)),
        out_shape=jax.ShapeDtypeStruct((n, ch), BF16),
        scratch_shapes=[pltpu.VMEM((ncc, CONV_HALO + ts, CONV_CW), F32),
                        pltpu.VMEM((ncc, ts, CONV_CW), F32)],
        compiler_params=_cparams("parallel", "arbitrary"), name="conv_module")(p, p, w4, b_dw, g_ln, b_ln)


def _merge_kernel(o_ref, c_ref, wa_ref, wc_ref, ga_ref, gb_ref, out_ref):
    ya = jnp.dot(o_ref[...], wa_ref[...], preferred_element_type=F32)
    yb = jnp.dot(c_ref[...], wc_ref[...], preferred_element_type=F32)
    out = jax.nn.sigmoid(ga_ref[...].astype(F32)) * ya + jax.nn.sigmoid(gb_ref[...].astype(F32)) * yb
    out_ref[...] = out.astype(out_ref.dtype)


def _merge(o, c, wa, wc, p, tm, tn):
    n, d = o.shape
    nj = d // tn
    return pl.pallas_call(
        _merge_kernel, grid=(n // tm, nj),
        in_specs=[pl.BlockSpec((tm, d), lambda i, j: (i, 0)),
                  pl.BlockSpec((tm, d), lambda i, j: (i, 0)),
                  pl.BlockSpec((d, tn), lambda i, j: (0, j)),
                  pl.BlockSpec((d, tn), lambda i, j: (0, j)),
                  pl.BlockSpec((tm, tn), lambda i, j: (i, 3 * nj + j)),
                  pl.BlockSpec((tm, tn), lambda i, j: (i, 4 * nj + j))],
        out_specs=pl.BlockSpec((tm, tn), lambda i, j: (i, j)),
        out_shape=jax.ShapeDtypeStruct((n, d), BF16),
        compiler_params=_cparams("parallel", "arbitrary"), name="merge")(o, c, wa, wc, p, p)


def _store_row_major(ref, val, rows):
    ns = val.shape[1] // LANES
    for s in range(ns):
        ref[pl.ds(s, rows, stride=ns), :] = val[:, s * LANES:(s + 1) * LANES]


def _res_router_kernel(m_ref, x_ref, w_ref, g_ref, wrh_ref, wrl_ref, br_ref,
                       x2_ref, t_ref, idx_ref, gate_ref, *, tm):
    x2 = x_ref[...] + jnp.dot(m_ref[...], w_ref[...], preferred_element_type=F32)
    x2_ref[...] = x2
    t = _rms(x2, g_ref[...])
    _store_row_major(t_ref, t, tm)
    t_hi = t.astype(BF16)
    t_lo = (t - t_hi.astype(F32)).astype(BF16)
    logits = (jnp.dot(t_hi, wrh_ref[...], preferred_element_type=F32)
              + jnp.dot(t_lo, wrh_ref[...], preferred_element_type=F32)
              + jnp.dot(t_hi, wrl_ref[...], preferred_element_type=F32)) + br_ref[...]
    ne = logits.shape[1]
    lane = lax.broadcasted_iota(jnp.int32, logits.shape, 1)
    k_lane = lax.broadcasted_iota(jnp.int32, (tm, TOP_K), 1)
    idx_out = jnp.zeros((tm, TOP_K), jnp.int32)
    val_out = jnp.zeros((tm, TOP_K), F32)
    cur = logits
    for kk in range(TOP_K):
        mx = jnp.max(cur, axis=-1, keepdims=True)
        am = jnp.min(jnp.where(cur == mx, lane, ne), axis=-1, keepdims=True)
        idx_out = jnp.where(k_lane == kk, am, idx_out)
        val_out = jnp.where(k_lane == kk, mx, val_out)
        cur = jnp.where(lane == am, -jnp.inf, cur)
    e = jnp.exp(val_out - val_out[:, 0:1])
    gate_ref[...] = e / jnp.sum(e, axis=-1, keepdims=True)
    idx_ref[...] = idx_out


def _res_router(merged, x2d, w_out, g_ffn, wr_hi, wr_lo, b_r, tm):
    n, d = x2d.shape
    ne = wr_hi.shape[1]
    ns = d // LANES
    rowd = pl.BlockSpec((tm, d), lambda i: (i, 0))
    rowk = pl.BlockSpec((tm, TOP_K), lambda i: (i, 0))
    const = lambda shape: pl.BlockSpec(shape, lambda i: (0, 0))
    return pl.pallas_call(
        functools.partial(_res_router_kernel, tm=tm), grid=(n // tm,),
        in_specs=[rowd, rowd, const((d, d)), const((1, d)), const((d, ne)), const((d, ne)), const((1, ne))],
        out_specs=[rowd, pl.BlockSpec((tm * ns, LANES), lambda i: (i, 0)), rowk, rowk],
        out_shape=[jax.ShapeDtypeStruct((n, d), F32),
                   jax.ShapeDtypeStruct((n * ns, LANES), F32),
                   jax.ShapeDtypeStruct((n, TOP_K), jnp.int32),
                   jax.ShapeDtypeStruct((n, TOP_K), F32)],
        compiler_params=_cparams("parallel"), name="res_router")(merged, x2d, w_out, g_ffn, wr_hi, wr_lo, b_r)


def _gather_rows(i, n, idx_hbm, src_hbm, ibuf, isem, buf, sem, rows, ns):
    def idx_copy(tile, slot):
        return pltpu.make_async_copy(idx_hbm.at[pl.ds(pl.multiple_of(tile * SUBLANES, SUBLANES), SUBLANES)],
                                     ibuf.at[slot], isem.at[slot])

    def issue_rows(slot):
        def body(r, carry):
            pltpu.make_async_copy(src_hbm.at[ibuf[slot, 0, r]],
                                  buf.at[slot, pl.ds(pl.multiple_of(r * ns, ns), ns)],
                                  sem.at[slot]).start()
            return carry
        lax.fori_loop(0, rows, body, 0)

    @pl.when(i == 0)
    def _():
        idx_copy(0, 0).start()
        idx_copy(0, 0).wait()
        issue_rows(0)

        @pl.when(n > 1)
        def _():
            idx_copy(1, 1).start()

    @pl.when(i + 1 < n)
    def _():
        nslot = (i + 1) % 2
        idx_copy(i + 1, nslot).wait()
        issue_rows(nslot)

    @pl.when(i + 2 < n)
    def _():
        idx_copy(i + 2, i % 2).start()

    slot = i % 2
    pltpu.make_async_copy(buf.at[slot], buf.at[slot], sem.at[slot]).wait()
    return slot


def _dispatch_kernel(idx_hbm, t_hbm, o_ref, ibuf, isem, buf, sem, *, tm, ns):
    i = pl.program_id(0)
    slot = _gather_rows(i, pl.num_programs(0), idx_hbm, t_hbm, ibuf, isem, buf, sem, tm, ns)
    for s in range(ns):
        o_ref[:, s * LANES:(s + 1) * LANES] = buf[slot, pl.ds(s, tm, stride=ns), :].astype(o_ref.dtype)


def _dispatch(idx8, t3, n_slots, tm):
    ns = t3.shape[1]
    return pl.pallas_call(
        functools.partial(_dispatch_kernel, tm=tm, ns=ns), grid=(n_slots // tm,),
        in_specs=[pl.BlockSpec(memory_space=pl.ANY), pl.BlockSpec(memory_space=pl.ANY)],
        out_specs=pl.BlockSpec((tm, ns * LANES), lambda i: (i, 0)),
        out_shape=jax.ShapeDtypeStruct((n_slots, ns * LANES), BF16),
        scratch_shapes=[pltpu.SMEM((2, SUBLANES, tm), jnp.int32), pltpu.SemaphoreType.DMA((2,)),
                        pltpu.VMEM((2, tm * ns, LANES), F32), pltpu.SemaphoreType.DMA((2,))],
        compiler_params=_cparams("arbitrary"), name="dispatch")(idx8, t3)


def _moe_up_kernel(te_ref, na_ref, x_ref, wg_ref, wu_ref, bg_ref, bu_ref, o_ref):
    active = pl.program_id(1) < na_ref[0]

    @pl.when(active)
    def _():
        x = x_ref[...]
        g = jnp.dot(x, wg_ref[0], preferred_element_type=F32) + bg_ref[0]
        u = jnp.dot(x, wu_ref[0], preferred_element_type=F32) + bu_ref[0]
        g = jnp.minimum(g, SWIGLU_LIMIT)
        u = jnp.clip(u, -SWIGLU_LIMIT, SWIGLU_LIMIT)
        o_ref[...] = ((u + 1.0) * (g * jax.nn.sigmoid(SWIGLU_ALPHA * g))).astype(o_ref.dtype)

    @pl.when(jnp.logical_not(active))
    def _():
        o_ref[...] = jnp.zeros(o_ref.shape, o_ref.dtype)


def _moe_up(tile_expert, n_active, xs, w_gu, b_gu, tm, tf):
    n_slots, d = xs.shape
    f = w_gu.shape[2] // 2
    nf = f // tf
    grid_spec = pltpu.PrefetchScalarGridSpec(
        num_scalar_prefetch=2, grid=(nf, n_slots // tm),
        in_specs=[pl.BlockSpec((tm, d), lambda j, i, te, na: (i, 0)),
                  pl.BlockSpec((1, d, tf), lambda j, i, te, na: (te[i], 0, j)),
                  pl.BlockSpec((1, d, tf), lambda j, i, te, na: (te[i], 0, nf + j)),
                  pl.BlockSpec((1, 1, tf), lambda j, i, te, na: (te[i], 0, j)),
                  pl.BlockSpec((1, 1, tf), lambda j, i, te, na: (te[i], 0, nf + j))],
        out_specs=pl.BlockSpec((tm, tf), lambda j, i, te, na: (i, j)))
    return pl.pallas_call(
        _moe_up_kernel, grid_spec=grid_spec,
        out_shape=jax.ShapeDtypeStruct((n_slots, f), BF16),
        compiler_params=_cparams("arbitrary", "arbitrary"), name="moe_up")(
            tile_expert, n_active, xs, w_gu, w_gu, b_gu, b_gu)


def _moe_down_kernel(te_ref, na_ref, h_ref, w_ref, b_ref, y_ref, *, tm):
    active = pl.program_id(0) < na_ref[0]

    @pl.when(active)
    def _():
        y = jnp.dot(h_ref[...], w_ref[0], preferred_element_type=F32) + b_ref[0]
        _store_row_major(y_ref, y, tm)

    @pl.when(jnp.logical_not(active))
    def _():
        y_ref[...] = jnp.zeros(y_ref.shape, y_ref.dtype)


def _moe_down(tile_expert, n_active, hdn, w_d, b_d, tm):
    n_slots, f = hdn.shape
    d = w_d.shape[2]
    ns = d // LANES
    grid_spec = pltpu.PrefetchScalarGridSpec(
        num_scalar_prefetch=2, grid=(n_slots // tm,),
        in_specs=[pl.BlockSpec((tm, f), lambda i, te, na: (i, 0)),
                  pl.BlockSpec((1, f, d), lambda i, te, na: (te[i], 0, 0)),
                  pl.BlockSpec((1, 1, d), lambda i, te, na: (te[i], 0, 0))],
        out_specs=pl.BlockSpec((tm * ns, LANES), lambda i, te, na: (i, 0)))
    return pl.pallas_call(
        functools.partial(_moe_down_kernel, tm=tm), grid_spec=grid_spec,
        out_shape=jax.ShapeDtypeStruct((n_slots * ns, LANES), F32),
        compiler_params=_cparams("arbitrary"), name="moe_down")(tile_expert, n_active, hdn, w_d, b_d)


def _combine_kernel(idx_hbm, y_hbm, x2_ref, gate_ref, g_ref, o_ref, ibuf, isem, buf, sem, m_ref, *, tc, ns):
    i = pl.program_id(0)
    slot = _gather_rows(i, pl.num_programs(0), idx_hbm, y_hbm, ibuf, isem, buf, sem, tc * TOP_K, ns)
    gate = gate_ref[...]
    for s in range(ns):
        acc = None
        for kk in range(TOP_K):
            slab = buf[slot, pl.ds(kk * ns + s, tc, stride=TOP_K * ns), :]
            term = gate[:, kk:kk + 1] * slab
            acc = term if acc is None else acc + term
        m_ref[:, s * LANES:(s + 1) * LANES] = acc
    o_ref[...] = _rms(x2_ref[...] + m_ref[...], g_ref[...])


def _combine(idx8, y3, x2, gates, g_final, tc):
    n, d = x2.shape
    ns = y3.shape[1]
    rows = tc * TOP_K
    return pl.pallas_call(
        functools.partial(_combine_kernel, tc=tc, ns=ns), grid=(n // tc,),
        in_specs=[pl.BlockSpec(memory_space=pl.ANY), pl.BlockSpec(memory_space=pl.ANY),
                  pl.BlockSpec((tc, d), lambda i: (i, 0)),
                  pl.BlockSpec((tc, TOP_K), lambda i: (i, 0)),
                  pl.BlockSpec((1, d), lambda i: (0, 0))],
        out_specs=pl.BlockSpec((tc, d), lambda i: (i, 0)),
        out_shape=jax.ShapeDtypeStruct((n, d), F32),
        scratch_shapes=[pltpu.SMEM((2, SUBLANES, rows), jnp.int32), pltpu.SemaphoreType.DMA((2,)),
                        pltpu.VMEM((2, rows * ns, LANES), F32), pltpu.SemaphoreType.DMA((2,)),
                        pltpu.VMEM((tc, d), F32)],
        compiler_params=_cparams("arbitrary"), name="combine")(idx8, y3, x2, gates, g_final)


def _index_rows(idx, rows):
    t = idx.shape[0] // rows
    return jnp.pad(idx.reshape(t, 1, rows), ((0, 0), (0, SUBLANES - 1), (0, 0))).reshape(t * SUBLANES, rows)


def _routing_plan(top_idx, tm):
    n, k = top_idx.shape
    a = n * k
    e_flat = top_idx.reshape(a)
    onehot = (e_flat[:, None] == jnp.arange(N_EXPERTS, dtype=jnp.int32)[None, :]).astype(jnp.int32)
    csum = jnp.cumsum(onehot, axis=0)
    rank = jnp.sum(onehot * csum, axis=1) - 1
    counts = csum[-1]
    tiles_e = (counts + tm - 1) // tm
    tile_end = jnp.cumsum(tiles_e)
    group_start = (tile_end - tiles_e) * tm
    slot = (group_start[e_flat] + rank).astype(jnp.int32)
    n_slots = a + N_EXPERTS * tm
    n_tiles = n_slots // tm
    slot_token = jnp.zeros((n_slots,), jnp.int32).at[slot].set(jnp.arange(a, dtype=jnp.int32) // k)
    n_active = tile_end[-1].astype(jnp.int32)
    tile_id = jnp.minimum(jnp.arange(n_tiles, dtype=jnp.int32), n_active - 1)
    tile_expert = jnp.minimum(jnp.searchsorted(tile_end, tile_id, side="right"), N_EXPERTS - 1).astype(jnp.int32)
    return slot, slot_token, tile_expert, n_active.reshape(1), n_slots


def _layout_w_in(w, d):
    o_q = Q_LORA_RANK
    o_kv = o_q + KV_LORA_RANK
    o_kr = o_kv + QK_ROPE_DIM
    z = lambda c: jnp.zeros((w.shape[0], c), w.dtype)
    pad_kr = Q_LORA_RANK - KV_LORA_RANK - QK_ROPE_DIM
    mla = jnp.concatenate([w[:, o_q:o_kv], w[:, o_kv:o_kr], z(pad_kr), w[:, :o_q], z(d - 2 * Q_LORA_RANK)], axis=1)
    return jnp.concatenate([mla, w[:, o_kr:]], axis=1).astype(BF16)


def _layout_w_q(w):
    r = w.shape[0]
    w3 = w.reshape(r, N_HEADS, QK_NOPE_DIM + QK_ROPE_DIM)
    w3 = jnp.pad(w3, ((0, 0), (0, 0), (0, HEAD_PAD - QK_NOPE_DIM - QK_ROPE_DIM)))
    return w3.reshape(r, N_HEADS * HEAD_PAD).astype(BF16)


def _layout_w_kv(w):
    r = w.shape[0]
    w3 = w.reshape(r, N_HEADS, QK_NOPE_DIM + V_HEAD_DIM)
    wk = w3[:, :, :QK_NOPE_DIM].reshape(r, N_HEADS * QK_NOPE_DIM)
    wv = w3[:, :, QK_NOPE_DIM:].reshape(r, N_HEADS * V_HEAD_DIM)
    return wk.astype(BF16), wv.astype(BF16)


def _layout_w_dw(w, ch):
    ncc = ch // CONV_CW
    w = jnp.pad(w, ((0, CONV_HALO - CONV_WIDTH), (0, 0)))
    return w.reshape(CONV_HALO, ncc, CONV_CW).transpose(1, 0, 2)


def _pick(n, pref):
    t = min(n, pref)
    while n % t:
        t //= 2
    return t


def kernel(x, positions, g_mix, w_in, g_q, w_q_up, g_kv, w_kv_up, w_attn_o, w_dw, b_dw, g_conv_ln, b_conv_ln,
           w_conv_out, w_out, g_ffn, w_router, b_router, w_gate_up, b_gate_up, w_down, b_down, g_final):
    batch, seq, d = x.shape
    n = batch * seq
    assert g_mix.shape[0] == 1, "one decoder layer"
    l = 0
    xf = x.reshape(n, d)

    half = QK_ROPE_DIM // 2
    inv_freq = ROPE_THETA ** (-jnp.arange(0, QK_ROPE_DIM, 2, dtype=F32) / QK_ROPE_DIM)
    invf = jnp.concatenate([inv_freq, inv_freq, jnp.zeros((LANES - 2 * half,), F32)]).reshape(1, LANES)
    pos_b = jnp.broadcast_to(positions.reshape(n, 1), (n, LANES))
    cos_t, sin_t = _rope_tables(pos_b, invf, _pick(n, 1024))

    tm_moe = _pick(n * TOP_K, 256)
    p = _in_proj(xf, g_mix[l].reshape(1, d), _layout_w_in(w_in[l], d), _pick(n, 1024), 512)
    q = _q_up(p, g_q[l].reshape(1, -1), _layout_w_q(w_q_up[l]), cos_t, sin_t, _pick(n, 1024), 4)
    wk, wv = _layout_w_kv(w_kv_up[l])
    k, v = _kv_up(p, g_kv[l].reshape(1, -1), wk, wv, cos_t, sin_t, _pick(n, 1024), 4)
    o = _attention(q, k, v, batch, seq, _pick(seq, 256))
    c = _conv_module(p, _layout_w_dw(w_dw[l], d), b_dw[l].reshape(1, d), g_conv_ln[l].reshape(1, d),
                     b_conv_ln[l].reshape(1, d), batch, seq, d, _pick(seq, 256))
    merged = _merge(o, c, w_attn_o[l].astype(BF16), w_conv_out[l].astype(BF16), p, _pick(n, 1024), 512)
    wr = w_router[l]
    wr_hi = wr.astype(BF16)
    wr_lo = (wr - wr_hi.astype(F32)).astype(BF16)
    x2, t_rows, top_idx, gates = _res_router(merged, xf, w_out[l].astype(BF16), g_ffn[l].reshape(1, d),
                                             wr_hi, wr_lo, b_router[l].reshape(1, -1), _pick(n, 256))
    ns = d // LANES
    slot, slot_token, tile_expert, n_active, n_slots = _routing_plan(top_idx, tm_moe)
    xs = _dispatch(_index_rows(slot_token, tm_moe), t_rows.reshape(n, ns, LANES), n_slots, tm_moe)
    hdn = _moe_up(tile_expert, n_active, xs, w_gate_up[l].astype(BF16),
                  b_gate_up[l].reshape(N_EXPERTS, 1, -1), tm_moe, 1024)
    y_rows = _moe_down(tile_expert, n_active, hdn, w_down[l].astype(BF16),
                       b_down[l].reshape(N_EXPERTS, 1, -1), tm_moe)
    tc = _pick(n, 128)
    out = _combine(_index_rows(slot, tc * TOP_K), y_rows.reshape(n_slots, ns, LANES), x2, gates,
                   g_final.reshape(1, d), tc)
    return out.reshape(batch, seq, d)
```

```python
import functools

import jax
import jax.numpy as jnp
from jax import lax
from jax.experimental import pallas as pl
from jax.experimental.pallas import tpu as pltpu

N_HEADS = 16
QK_NOPE_DIM = 128
QK_ROPE_DIM = 64
V_HEAD_DIM = 128
Q_LORA_RANK = 768
KV_LORA_RANK = 512
ROPE_THETA = 10000.0
ATTN_SCALE = (QK_NOPE_DIM + QK_ROPE_DIM) ** -0.5
CONV_WIDTH = 31
N_EXPERTS = 32
TOP_K = 4
SWIGLU_LIMIT = 7.0
SWIGLU_ALPHA = 1.702
NORM_EPS = 1e-6

LANES = 128
SUBLANES = 8
HEAD_PAD = 2 * LANES
VMEM_LIMIT = 56 * 1024 * 1024

F32 = jnp.float32
BF16 = jnp.bfloat16
U32 = jnp.uint32


def _cparams(*sem):
    return pltpu.CompilerParams(dimension_semantics=sem, vmem_limit_bytes=VMEM_LIMIT)


def _rms(x, g):
    ms = jnp.mean(x * x, axis=-1, keepdims=True)
    return x * lax.rsqrt(ms + NORM_EPS) * g


def _rope_tables_kernel(pos_ref, invf_ref, cos_ref, sin_ref):
    ang = pos_ref[...].astype(F32) * invf_ref[...]
    lane = lax.broadcasted_iota(jnp.int32, ang.shape, 1)
    c = jnp.cos(ang)
    s = jnp.sin(ang)
    cos_ref[...] = jnp.where(lane < QK_ROPE_DIM, c, 0.0)
    sin_ref[...] = jnp.where(lane < QK_ROPE_DIM // 2, -s, jnp.where(lane < QK_ROPE_DIM, s, 0.0))


def _rope_tables(pos_b, invf, tm):
    n = pos_b.shape[0]
    row = pl.BlockSpec((tm, LANES), lambda i: (i, 0))
    return pl.pallas_call(
        _rope_tables_kernel, grid=(n // tm,),
        in_specs=[row, pl.BlockSpec((1, LANES), lambda i: (0, 0))],
        out_specs=[row, row],
        out_shape=[jax.ShapeDtypeStruct((n, LANES), F32)] * 2,
        compiler_params=_cparams("parallel"), name="rope_tables")(pos_b, invf)


def _rope(r, cos, sin):
    half = QK_ROPE_DIM // 2
    lane = lax.broadcasted_iota(jnp.int32, r.shape, 1)
    rot = jnp.where(lane < half, pltpu.roll(r, LANES - half, 1), pltpu.roll(r, half, 1))
    return r * cos + rot * sin


def _in_proj_kernel(x_ref, g_ref, w_ref, o_ref, h_ref):
    @pl.when(pl.program_id(1) == 0)
    def _():
        h_ref[...] = _rms(x_ref[...], g_ref[...]).astype(h_ref.dtype)

    o_ref[...] = jnp.dot(h_ref[...], w_ref[...], preferred_element_type=F32).astype(o_ref.dtype)


def _in_proj(x2d, g, w, tm, tn):
    n, d = x2d.shape
    cols = w.shape[1]
    return pl.pallas_call(
        _in_proj_kernel, grid=(n // tm, cols // tn),
        in_specs=[pl.BlockSpec((tm, d), lambda i, j: (i, 0)),
                  pl.BlockSpec((1, d), lambda i, j: (0, 0)),
                  pl.BlockSpec((d, tn), lambda i, j: (0, j))],
        out_specs=pl.BlockSpec((tm, tn), lambda i, j: (i, j)),
        out_shape=jax.ShapeDtypeStruct((n, cols), BF16),
        scratch_shapes=[pltpu.VMEM((tm, d), BF16)],
        compiler_params=_cparams("parallel", "arbitrary"), name="in_proj")(x2d, g, w)


def _q_up_kernel(c_ref, g_ref, w_ref, cos_ref, sin_ref, o_ref, cn_ref, *, hb):
    @pl.when(pl.program_id(1) == 0)
    def _():
        cn_ref[...] = _rms(c_ref[...].astype(F32), g_ref[...]).astype(cn_ref.dtype)

    q = jnp.dot(cn_ref[...], w_ref[...], preferred_element_type=F32)
    cos = cos_ref[...]
    sin = sin_ref[...]
    for h in range(hb):
        b0 = h * HEAD_PAD
        o_ref[:, b0:b0 + LANES] = (q[:, b0:b0 + LANES] * ATTN_SCALE).astype(o_ref.dtype)
        o_ref[:, b0 + LANES:b0 + HEAD_PAD] = (
            _rope(q[:, b0 + LANES:b0 + HEAD_PAD], cos, sin) * ATTN_SCALE).astype(o_ref.dtype)


def _q_up(p, g, w, cos_t, sin_t, tm, hb):
    n = p.shape[0]
    r = Q_LORA_RANK
    row = pl.BlockSpec((tm, LANES), lambda i, j: (i, 0))
    return pl.pallas_call(
        functools.partial(_q_up_kernel, hb=hb), grid=(n // tm, N_HEADS // hb),
        in_specs=[pl.BlockSpec((tm, r), lambda i, j: (i, 1)),
                  pl.BlockSpec((1, r), lambda i, j: (0, 0)),
                  pl.BlockSpec((r, hb * HEAD_PAD), lambda i, j: (0, j)),
                  row, row],
        out_specs=pl.BlockSpec((tm, hb * HEAD_PAD), lambda i, j: (i, j)),
        out_shape=jax.ShapeDtypeStruct((n, N_HEADS * HEAD_PAD), BF16),
        scratch_shapes=[pltpu.VMEM((tm, r), BF16)],
        compiler_params=_cparams("parallel", "arbitrary"), name="q_up")(p, g, w, cos_t, sin_t)


def _kv_up_kernel(c_ref, g_ref, wk_ref, wv_ref, cos_ref, sin_ref, k_ref, v_ref, cn_ref, kr_ref, *, hb):
    @pl.when(pl.program_id(1) == 0)
    def _():
        cn_ref[...] = _rms(c_ref[:, :KV_LORA_RANK].astype(F32), g_ref[...]).astype(cn_ref.dtype)
        kr = c_ref[:, KV_LORA_RANK:KV_LORA_RANK + LANES].astype(F32)
        kr_ref[...] = _rope(kr, cos_ref[...], sin_ref[...]).astype(kr_ref.dtype)

    cn = cn_ref[...]
    kn = jnp.dot(cn, wk_ref[...], preferred_element_type=F32)
    for h in range(hb):
        k_ref[:, h * HEAD_PAD:h * HEAD_PAD + LANES] = kn[:, h * LANES:(h + 1) * LANES].astype(k_ref.dtype)
        k_ref[:, h * HEAD_PAD + LANES:(h + 1) * HEAD_PAD] = kr_ref[...]
    v_ref[...] = jnp.dot(cn, wv_ref[...], preferred_element_type=F32).astype(v_ref.dtype)


def _kv_up(p, g, wk, wv, cos_t, sin_t, tm, hb):
    n = p.shape[0]
    r = KV_LORA_RANK
    cw = Q_LORA_RANK
    row = pl.BlockSpec((tm, LANES), lambda i, j: (i, 0))
    return pl.pallas_call(
        functools.partial(_kv_up_kernel, hb=hb), grid=(n // tm, N_HEADS // hb),
        in_specs=[pl.BlockSpec((tm, cw), lambda i, j: (i, 0)),
                  pl.BlockSpec((1, r), lambda i, j: (0, 0)),
                  pl.BlockSpec((r, hb * LANES), lambda i, j: (0, j)),
                  pl.BlockSpec((r, hb * LANES), lambda i, j: (0, j)),
                  row, row],
        out_specs=[pl.BlockSpec((tm, hb * HEAD_PAD), lambda i, j: (i, j)),
                   pl.BlockSpec((tm, hb * LANES), lambda i, j: (i, j))],
        out_shape=[jax.ShapeDtypeStruct((n, N_HEADS * HEAD_PAD), BF16),
                   jax.ShapeDtypeStruct((n, N_HEADS * V_HEAD_DIM), BF16)],
        scratch_shapes=[pltpu.VMEM((tm, r), BF16), pltpu.VMEM((tm, LANES), BF16)],
        compiler_params=_cparams("parallel", "arbitrary"), name="kv_up")(p, g, wk, wv, cos_t, sin_t)


def _attn_kernel(q_ref, k_ref, v_ref, o_ref, *, tq):
    s_len = q_ref.shape[0]
    dn = (((1,), (1,)), ((), ()))
    row = lax.broadcasted_iota(jnp.int32, (tq, tq), 0)
    col = lax.broadcasted_iota(jnp.int32, (tq, tq), 1)
    for i in range(s_len // tq):
        lo = i * tq
        q = q_ref[lo:lo + tq, :]
        sd = lax.dot_general(q, k_ref[lo:lo + tq, :], dn, preferred_element_type=F32)
        sd = jnp.where(col <= row, sd, -jnp.inf)
        m = jnp.max(sd, axis=-1, keepdims=True)
        if i > 0:
            sp = lax.dot_general(q, k_ref[0:lo, :], dn, preferred_element_type=F32)
            m = jnp.maximum(m, jnp.max(sp, axis=-1, keepdims=True))
            pp = jnp.exp(sp - m)
            l = jnp.sum(pp, axis=-1, keepdims=True)
            acc = jnp.dot(pp.astype(v_ref.dtype), v_ref[0:lo, :], preferred_element_type=F32)
        pd = jnp.exp(sd - m)
        ld = jnp.sum(pd, axis=-1, keepdims=True)
        accd = jnp.dot(pd.astype(v_ref.dtype), v_ref[lo:lo + tq, :], preferred_element_type=F32)
        if i > 0:
            ld = ld + l
            accd = accd + acc
        o_ref[lo:lo + tq, :] = (accd / ld).astype(o_ref.dtype)


def _attention(q, k, v, batch, seq, tq):
    n = q.shape[0]
    return pl.pallas_call(
        functools.partial(_attn_kernel, tq=tq), grid=(batch, N_HEADS),
        in_specs=[pl.BlockSpec((seq, HEAD_PAD), lambda b, h: (b, h)),
                  pl.BlockSpec((seq, HEAD_PAD), lambda b, h: (b, h)),
                  pl.BlockSpec((seq, V_HEAD_DIM), lambda b, h: (b, h))],
        out_specs=pl.BlockSpec((seq, V_HEAD_DIM), lambda b, h: (b, h)),
        out_shape=jax.ShapeDtypeStruct((n, N_HEADS * V_HEAD_DIM), BF16),
        compiler_params=_cparams("parallel", "parallel"), name="attention")(q, k, v)


CONV_HALO = 32
CONV_CW = 512
CONV_RC = 32


def _conv_kernel(a_ref, b_ref, w_ref, bdw_ref, g_ref, bln_ref, o_ref, gbuf, shbuf, cbuf, *, ts):
    si = pl.program_id(1)
    ch = a_ref.shape[1]
    ncc = ch // CONV_CW
    sh_rows = shbuf.shape[1]

    @pl.when(si == 0)
    def _():
        gbuf[:, 0:CONV_HALO, :] = jnp.zeros((ncc, CONV_HALO, CONV_CW), F32)

    @pl.when(si > 0)
    def _():
        gbuf[:, 0:CONV_HALO, :] = gbuf[:, ts:ts + CONV_HALO, :]

    for cc in range(ncc):
        a = a_ref[:, cc * CONV_CW:(cc + 1) * CONV_CW].astype(F32)
        b = b_ref[:, cc * CONV_CW:(cc + 1) * CONV_CW].astype(F32)
        gbuf[cc, CONV_HALO:CONV_HALO + ts, :] = a * jax.nn.sigmoid(b)

    def chunk(cc, carry):
        for j in range(1, SUBLANES):
            shbuf[j - 1] = gbuf[cc, j:j + sh_rows, :]
        for r in range(ts // CONV_RC):
            acc = jnp.zeros((CONV_RC, CONV_CW), F32)
            for kk in range(CONV_WIDTH):
                off = r * CONV_RC + CONV_HALO - (CONV_WIDTH - 1) + kk
                j = off % SUBLANES
                base = off - j
                rows = gbuf[cc, base:base + CONV_RC, :] if j == 0 else shbuf[j - 1, base:base + CONV_RC, :]
                acc = acc + w_ref[cc, kk:kk + 1, :] * rows
            cbuf[cc, r * CONV_RC:(r + 1) * CONV_RC, :] = acc
        return carry

    lax.fori_loop(0, ncc, chunk, 0)

    c = [cbuf[cc] + bdw_ref[:, cc * CONV_CW:(cc + 1) * CONV_CW] for cc in range(ncc)]
    mu = sum(jnp.sum(x, axis=-1, keepdims=True) for x in c) / ch
    var = sum(jnp.sum(jnp.square(x - mu), axis=-1, keepdims=True) for x in c) / ch
    rstd = lax.rsqrt(var + NORM_EPS)
    for cc in range(ncc):
        sl = slice(cc * CONV_CW, (cc + 1) * CONV_CW)
        y = (c[cc] - mu) * rstd * g_ref[:, sl] + bln_ref[:, sl]
        o_ref[:, sl] = (y * jax.nn.sigmoid(y)).astype(o_ref.dtype)


def _conv_module(p, w4, b_dw, g_ln, b_ln, batch, seq, ch, ts):
    n = p.shape[0]
    nseq = seq // ts
    ncc = ch // CONV_CW
    vec = pl.BlockSpec((1, ch), lambda b, s: (0, 0))
    return pl.pallas_call(
        functools.partial(_conv_kernel, ts=ts), grid=(batch, nseq),
        in_specs=[pl.BlockSpec((ts, ch), lambda b, s: (b * nseq + s, 1)),
                  pl.BlockSpec((ts, ch), lambda b, s: (b * nseq + s, 2)),
                  pl.BlockSpec((ncc, CONV_HALO, CONV_CW), lambda b, s: (0, 0, 0)),
                  vec, vec, vec],
        out_specs=pl.BlockSpec((ts, ch), lambda b, s: (b * nseq + s, 0)),
        out_shape=jax.ShapeDtypeStruct((n, ch), BF16),
        scratch_shapes=[pltpu.VMEM((ncc, CONV_HALO + ts, CONV_CW), F32),
                        pltpu.VMEM((SUBLANES - 1, CONV_HALO + ts - SUBLANES, CONV_CW), F32),
                        pltpu.VMEM((ncc, ts, CONV_CW), F32)],
        compiler_params=_cparams("parallel", "arbitrary"), name="conv_module")(p, p, w4, b_dw, g_ln, b_ln)


def _merge_kernel(o_ref, c_ref, wa_ref, wc_ref, ga_ref, gb_ref, out_ref):
    ya = jnp.dot(o_ref[...], wa_ref[...], preferred_element_type=F32)
    yb = jnp.dot(c_ref[...], wc_ref[...], preferred_element_type=F32)
    out = jax.nn.sigmoid(ga_ref[...].astype(F32)) * ya + jax.nn.sigmoid(gb_ref[...].astype(F32)) * yb
    out_ref[...] = out.astype(out_ref.dtype)


def _merge(o, c, wa, wc, p, tm, tn):
    n, d = o.shape
    nj = d // tn
    return pl.pallas_call(
        _merge_kernel, grid=(n // tm, nj),
        in_specs=[pl.BlockSpec((tm, d), lambda i, j: (i, 0)),
                  pl.BlockSpec((tm, d), lambda i, j: (i, 0)),
                  pl.BlockSpec((d, tn), lambda i, j: (0, j)),
                  pl.BlockSpec((d, tn), lambda i, j: (0, j)),
                  pl.BlockSpec((tm, tn), lambda i, j: (i, 3 * nj + j)),
                  pl.BlockSpec((tm, tn), lambda i, j: (i, 4 * nj + j))],
        out_specs=pl.BlockSpec((tm, tn), lambda i, j: (i, j)),
        out_shape=jax.ShapeDtypeStruct((n, d), BF16),
        compiler_params=_cparams("parallel", "arbitrary"), name="merge")(o, c, wa, wc, p, p)


HI_MASK = 0xFFFF0000


def _pack_rows(ref, lead, val, rows):
    half = val.shape[1] // 2
    ns = half // LANES
    lo = lax.bitcast_convert_type(val[:, :half].astype(BF16).astype(F32), U32) >> 16
    hi = lax.bitcast_convert_type(val[:, half:].astype(BF16).astype(F32), U32) & U32(HI_MASK)
    w = lo | hi
    for s in range(ns):
        ref[lead + (pl.ds(s, rows, stride=ns), slice(None))] = w[:, s * LANES:(s + 1) * LANES]


def _unpack_slab(words):
    lo = lax.bitcast_convert_type(words << 16, F32)
    hi = lax.bitcast_convert_type(words & U32(HI_MASK), F32)
    return lo, hi


def _res_router_kernel(m_ref, x_ref, w_ref, g_ref, wrh_ref, wrl_ref, br_ref,
                       x2_ref, t_ref, idx_ref, gate_ref, *, tm):
    x2 = x_ref[...] + jnp.dot(m_ref[...], w_ref[...], preferred_element_type=F32)
    x2_ref[...] = x2
    t = _rms(x2, g_ref[...])
    _pack_rows(t_ref, (), t, tm)
    t_hi = t.astype(BF16)
    t_lo = (t - t_hi.astype(F32)).astype(BF16)
    logits = (jnp.dot(t_hi, wrh_ref[...], preferred_element_type=F32)
              + jnp.dot(t_lo, wrh_ref[...], preferred_element_type=F32)
              + jnp.dot(t_hi, wrl_ref[...], preferred_element_type=F32)) + br_ref[...]
    ne = logits.shape[1]
    lane = lax.broadcasted_iota(jnp.int32, logits.shape, 1)
    k_lane = lax.broadcasted_iota(jnp.int32, (tm, TOP_K), 1)
    idx_out = jnp.zeros((tm, TOP_K), jnp.int32)
    val_out = jnp.zeros((tm, TOP_K), F32)
    cur = logits
    for kk in range(TOP_K):
        mx = jnp.max(cur, axis=-1, keepdims=True)
        am = jnp.min(jnp.where(cur == mx, lane, ne), axis=-1, keepdims=True)
        idx_out = jnp.where(k_lane == kk, am, idx_out)
        val_out = jnp.where(k_lane == kk, mx, val_out)
        cur = jnp.where(lane == am, -jnp.inf, cur)
    e = jnp.exp(val_out - val_out[:, 0:1])
    gate_ref[...] = e / jnp.sum(e, axis=-1, keepdims=True)
    idx_ref[...] = idx_out


def _res_router(merged, x2d, w_out, g_ffn, wr_hi, wr_lo, b_r, tm):
    n, d = x2d.shape
    ne = wr_hi.shape[1]
    ns = d // (2 * LANES)
    rowd = pl.BlockSpec((tm, d), lambda i: (i, 0))
    rowk = pl.BlockSpec((tm, TOP_K), lambda i: (i, 0))
    const = lambda shape: pl.BlockSpec(shape, lambda i: (0, 0))
    return pl.pallas_call(
        functools.partial(_res_router_kernel, tm=tm), grid=(n // tm,),
        in_specs=[rowd, rowd, const((d, d)), const((1, d)), const((d, ne)), const((d, ne)), const((1, ne))],
        out_specs=[rowd, pl.BlockSpec((tm * ns, LANES), lambda i: (i, 0)), rowk, rowk],
        out_shape=[jax.ShapeDtypeStruct((n, d), F32),
                   jax.ShapeDtypeStruct((n * ns, LANES), U32),
                   jax.ShapeDtypeStruct((n, TOP_K), jnp.int32),
                   jax.ShapeDtypeStruct((n, TOP_K), F32)],
        compiler_params=_cparams("parallel"), name="res_router")(merged, x2d, w_out, g_ffn, wr_hi, wr_lo, b_r)


def _idx_copy(idx_hbm, ibuf, isem, tile, slot):
    return pltpu.make_async_copy(idx_hbm.at[pl.ds(pl.multiple_of(tile * SUBLANES, SUBLANES), SUBLANES)],
                                 ibuf.at[slot], isem.at[slot])


def _wait_all_rows(buf, sem, slot):
    pltpu.make_async_copy(buf.at[slot], buf.at[slot], sem.at[slot]).wait()


def _gather_rows(i, n, idx_hbm, src_hbm, ibuf, isem, buf, sem, rows, ns):
    def issue_rows(slot):
        def body(r, carry):
            pltpu.make_async_copy(src_hbm.at[ibuf[slot, 0, r]],
                                  buf.at[slot, pl.ds(pl.multiple_of(r * ns, ns), ns)],
                                  sem.at[slot]).start()
            return carry
        lax.fori_loop(0, rows, body, 0)

    @pl.when(i == 0)
    def _():
        _idx_copy(idx_hbm, ibuf, isem, 0, 0).start()
        _idx_copy(idx_hbm, ibuf, isem, 0, 0).wait()
        issue_rows(0)

        @pl.when(n > 1)
        def _():
            _idx_copy(idx_hbm, ibuf, isem, 1, 1).start()

    @pl.when(i + 1 < n)
    def _():
        nslot = (i + 1) % 2
        _idx_copy(idx_hbm, ibuf, isem, i + 1, nslot).wait()
        issue_rows(nslot)

    @pl.when(i + 2 < n)
    def _():
        _idx_copy(idx_hbm, ibuf, isem, i + 2, i % 2).start()

    slot = i % 2
    _wait_all_rows(buf, sem, slot)
    return slot


def _moe_up_kernel(te_ref, na_ref, idx_hbm, t_hbm, w_ref, b_ref, o_ref, ibuf, isem, buf, sem, x_ref, *, tm, tf):
    i = pl.program_id(0)
    n_act = na_ref[0]
    f = o_ref.shape[1]
    half = x_ref.shape[1] // 2

    @pl.when(i < n_act)
    def _():
        slot = _gather_rows(i, n_act, idx_hbm, t_hbm, ibuf, isem, buf, sem, tm, SUBLANES)
        for s in range(SUBLANES):
            lo, hi = _unpack_slab(buf[slot, pl.ds(s, tm, stride=SUBLANES), :])
            x_ref[:, s * LANES:(s + 1) * LANES] = lo.astype(x_ref.dtype)
            x_ref[:, half + s * LANES:half + (s + 1) * LANES] = hi.astype(x_ref.dtype)
        x = x_ref[...]
        for c in range(f // tf):
            gs = slice(c * tf, (c + 1) * tf)
            us = slice(f + c * tf, f + (c + 1) * tf)
            g = jnp.dot(x, w_ref[0, :, gs], preferred_element_type=F32) + b_ref[0, :, gs]
            u = jnp.dot(x, w_ref[0, :, us], preferred_element_type=F32) + b_ref[0, :, us]
            g = jnp.minimum(g, SWIGLU_LIMIT)
            u = jnp.clip(u, -SWIGLU_LIMIT, SWIGLU_LIMIT)
            o_ref[:, gs] = ((u + 1.0) * (g * jax.nn.sigmoid(SWIGLU_ALPHA * g))).astype(o_ref.dtype)

    @pl.when(i >= n_act)
    def _():
        o_ref[...] = jnp.zeros(o_ref.shape, o_ref.dtype)


def _moe_up(tile_expert, n_active, tok8, t3, w_gu, b_gu, n_slots, tm, tf):
    d = w_gu.shape[1]
    f = w_gu.shape[2] // 2
    grid_spec = pltpu.PrefetchScalarGridSpec(
        num_scalar_prefetch=2, grid=(n_slots // tm,),
        in_specs=[pl.BlockSpec(memory_space=pl.ANY), pl.BlockSpec(memory_space=pl.ANY),
                  pl.BlockSpec((1, d, 2 * f), lambda i, te, na: (te[i], 0, 0)),
                  pl.BlockSpec((1, 1, 2 * f), lambda i, te, na: (te[i], 0, 0))],
        out_specs=pl.BlockSpec((tm, f), lambda i, te, na: (i, 0)),
        scratch_shapes=[pltpu.SMEM((2, SUBLANES, tm), jnp.int32), pltpu.SemaphoreType.DMA((2,)),
                        pltpu.VMEM((2, tm * SUBLANES, LANES), U32), pltpu.SemaphoreType.DMA((2,)),
                        pltpu.VMEM((tm, d), BF16)])
    return pl.pallas_call(
        functools.partial(_moe_up_kernel, tm=tm, tf=tf), grid_spec=grid_spec,
        out_shape=jax.ShapeDtypeStruct((n_slots, f), BF16),
        compiler_params=_cparams("arbitrary"), name="moe_up")(tile_expert, n_active, tok8, t3, w_gu, b_gu)


def _moe_down_kernel(te_ref, na_ref, dst_hbm, y_init, h_ref, w_ref, b_ref, y_hbm, ibuf, isem, stage, sem, *, tm):
    del y_init
    i = pl.program_id(0)
    n_act = na_ref[0]

    @pl.when(i < n_act)
    def _():
        slot = i % 2

        @pl.when(i == 0)
        def _():
            _idx_copy(dst_hbm, ibuf, isem, 0, 0).start()

        @pl.when(i + 1 < n_act)
        def _():
            _idx_copy(dst_hbm, ibuf, isem, i + 1, 1 - slot).start()

        @pl.when(i >= 2)
        def _():
            _wait_all_rows(stage, sem, slot)

        y = jnp.dot(h_ref[...], w_ref[0], preferred_element_type=F32) + b_ref[0]
        _pack_rows(stage, (slot,), y, tm)
        _idx_copy(dst_hbm, ibuf, isem, i, slot).wait()

        def body(r, carry):
            pltpu.make_async_copy(stage.at[slot, pl.ds(pl.multiple_of(r * SUBLANES, SUBLANES), SUBLANES)],
                                  y_hbm.at[ibuf[slot, 0, r]], sem.at[slot]).start()
            return carry
        lax.fori_loop(0, tm, body, 0)

        @pl.when(i == n_act - 1)
        def _():
            _wait_all_rows(stage, sem, slot)

            @pl.when(i >= 1)
            def _():
                _wait_all_rows(stage, sem, 1 - slot)


def _moe_down(tile_expert, n_active, dst8, hdn, w_d, b_d, n_rows, tm):
    n_slots, f = hdn.shape
    d = w_d.shape[2]
    grid_spec = pltpu.PrefetchScalarGridSpec(
        num_scalar_prefetch=2, grid=(n_slots // tm,),
        in_specs=[pl.BlockSpec(memory_space=pl.ANY), pl.BlockSpec(memory_space=pl.ANY),
                  pl.BlockSpec((tm, f), lambda i, te, na: (i, 0)),
                  pl.BlockSpec((1, f, d), lambda i, te, na: (te[i], 0, 0)),
                  pl.BlockSpec((1, 1, d), lambda i, te, na: (te[i], 0, 0))],
        out_specs=pl.BlockSpec(memory_space=pl.ANY),
        scratch_shapes=[pltpu.SMEM((2, SUBLANES, tm), jnp.int32), pltpu.SemaphoreType.DMA((2,)),
                        pltpu.VMEM((2, tm * SUBLANES, LANES), U32), pltpu.SemaphoreType.DMA((2,))])
    y_init = jnp.zeros((n_rows, SUBLANES, LANES), U32)
    return pl.pallas_call(
        functools.partial(_moe_down_kernel, tm=tm), grid_spec=grid_spec,
        out_shape=jax.ShapeDtypeStruct((n_rows, SUBLANES, LANES), U32),
        input_output_aliases={3: 0},
        compiler_params=_cparams("arbitrary"), name="moe_down")(tile_expert, n_active, dst8, y_init, hdn, w_d, b_d)


def _combine_kernel(y0_ref, y1_ref, y2_ref, y3_ref, x2_ref, gate_ref, g_ref, o_ref, m_ref, *, tc):
    half = m_ref.shape[1] // 2
    gate = gate_ref[...]
    gk = [jnp.broadcast_to(gate[:, kk:kk + 1], (tc, LANES)) for kk in range(TOP_K)]
    for s in range(SUBLANES):
        lo_acc = hi_acc = None
        for kk, y_ref in enumerate((y0_ref, y1_ref, y2_ref, y3_ref)):
            lo, hi = _unpack_slab(y_ref[pl.ds(s, tc, stride=SUBLANES), :])
            lo_acc = gk[kk] * lo if lo_acc is None else lo_acc + gk[kk] * lo
            hi_acc = gk[kk] * hi if hi_acc is None else hi_acc + gk[kk] * hi
        m_ref[:, s * LANES:(s + 1) * LANES] = lo_acc
        m_ref[:, half + s * LANES:half + (s + 1) * LANES] = hi_acc
    o_ref[...] = _rms(x2_ref[...] + m_ref[...], g_ref[...])


def _combine(y_rows, x2, gates, g_final, tc):
    n, d = x2.shape
    nt = n // tc
    y2d = y_rows.reshape(y_rows.shape[0] * SUBLANES, LANES)
    y_specs = [pl.BlockSpec((tc * SUBLANES, LANES), lambda i, kk=kk: (kk * nt + i, 0)) for kk in range(TOP_K)]
    return pl.pallas_call(
        functools.partial(_combine_kernel, tc=tc), grid=(nt,),
        in_specs=y_specs + [pl.BlockSpec((tc, d), lambda i: (i, 0)),
                            pl.BlockSpec((tc, TOP_K), lambda i: (i, 0)),
                            pl.BlockSpec((1, d), lambda i: (0, 0))],
        out_specs=pl.BlockSpec((tc, d), lambda i: (i, 0)),
        out_shape=jax.ShapeDtypeStruct((n, d), F32),
        scratch_shapes=[pltpu.VMEM((tc, d), F32)],
        compiler_params=_cparams("parallel"), name="combine")(y2d, y2d, y2d, y2d, x2, gates, g_final)


def _index_rows(idx, rows):
    t = idx.shape[0] // rows
    return jnp.pad(idx.reshape(t, 1, rows), ((0, 0), (0, SUBLANES - 1), (0, 0))).reshape(t * SUBLANES, rows)


def _routing_plan(top_idx, tm):
    n, k = top_idx.shape
    a = n * k
    e_flat = top_idx.reshape(a)
    onehot = (e_flat[:, None] == jnp.arange(N_EXPERTS, dtype=jnp.int32)[None, :]).astype(jnp.int32)
    csum = jnp.cumsum(onehot, axis=0)
    rank = jnp.sum(onehot * csum, axis=1) - 1
    counts = csum[-1]
    tiles_e = (counts + tm - 1) // tm
    tile_end = jnp.cumsum(tiles_e)
    group_start = (tile_end - tiles_e) * tm
    slot = (group_start[e_flat] + rank).astype(jnp.int32)
    n_slots = a + N_EXPERTS * tm
    n_tiles = n_slots // tm
    slot_a = jnp.full((n_slots,), -1, jnp.int32).at[slot].set(jnp.arange(a, dtype=jnp.int32))
    valid = slot_a >= 0
    tok = jnp.where(valid, slot_a // k, 0)
    dst = jnp.where(valid, (slot_a % k) * n + slot_a // k, a + jnp.arange(n_slots, dtype=jnp.int32))
    n_active = tile_end[-1].astype(jnp.int32)
    tile_id = jnp.minimum(jnp.arange(n_tiles, dtype=jnp.int32), n_active - 1)
    tile_expert = jnp.minimum(jnp.searchsorted(tile_end, tile_id, side="right"), N_EXPERTS - 1).astype(jnp.int32)
    return tok, dst, tile_expert, n_active.reshape(1), n_slots


def _layout_w_in(w, d):
    o_q = Q_LORA_RANK
    o_kv = o_q + KV_LORA_RANK
    o_kr = o_kv + QK_ROPE_DIM
    z = lambda c: jnp.zeros((w.shape[0], c), w.dtype)
    pad_kr = Q_LORA_RANK - KV_LORA_RANK - QK_ROPE_DIM
    mla = jnp.concatenate([w[:, o_q:o_kv], w[:, o_kv:o_kr], z(pad_kr), w[:, :o_q], z(d - 2 * Q_LORA_RANK)], axis=1)
    return jnp.concatenate([mla, w[:, o_kr:]], axis=1).astype(BF16)


def _layout_w_q(w):
    r = w.shape[0]
    w3 = w.reshape(r, N_HEADS, QK_NOPE_DIM + QK_ROPE_DIM)
    w3 = jnp.pad(w3, ((0, 0), (0, 0), (0, HEAD_PAD - QK_NOPE_DIM - QK_ROPE_DIM)))
    return w3.reshape(r, N_HEADS * HEAD_PAD).astype(BF16)


def _layout_w_kv(w):
    r = w.shape[0]
    w3 = w.reshape(r, N_HEADS, QK_NOPE_DIM + V_HEAD_DIM)
    wk = w3[:, :, :QK_NOPE_DIM].reshape(r, N_HEADS * QK_NOPE_DIM)
    wv = w3[:, :, QK_NOPE_DIM:].reshape(r, N_HEADS * V_HEAD_DIM)
    return wk.astype(BF16), wv.astype(BF16)


def _layout_w_dw(w, ch):
    ncc = ch // CONV_CW
    w = jnp.pad(w, ((0, CONV_HALO - CONV_WIDTH), (0, 0)))
    return w.reshape(CONV_HALO, ncc, CONV_CW).transpose(1, 0, 2)


def _pick(n, pref):
    t = min(n, pref)
    while n % t:
        t //= 2
    return t


def kernel(x, positions, g_mix, w_in, g_q, w_q_up, g_kv, w_kv_up, w_attn_o, w_dw, b_dw, g_conv_ln, b_conv_ln,
           w_conv_out, w_out, g_ffn, w_router, b_router, w_gate_up, b_gate_up, w_down, b_down, g_final):
    batch, seq, d = x.shape
    n = batch * seq
    assert g_mix.shape[0] == 1, "one decoder layer"
    assert d == 2 * SUBLANES * LANES, "a packed token row is one (8, 128) tile"
    l = 0
    xf = x.reshape(n, d)

    half = QK_ROPE_DIM // 2
    inv_freq = ROPE_THETA ** (-jnp.arange(0, QK_ROPE_DIM, 2, dtype=F32) / QK_ROPE_DIM)
    invf = jnp.concatenate([inv_freq, inv_freq, jnp.zeros((LANES - 2 * half,), F32)]).reshape(1, LANES)
    pos_b = jnp.broadcast_to(positions.reshape(n, 1), (n, LANES))
    cos_t, sin_t = _rope_tables(pos_b, invf, _pick(n, 1024))

    tm_moe = _pick(n * TOP_K, 256)
    p = _in_proj(xf, g_mix[l].reshape(1, d), _layout_w_in(w_in[l], d), _pick(n, 1024), 512)
    q = _q_up(p, g_q[l].reshape(1, -1), _layout_w_q(w_q_up[l]), cos_t, sin_t, _pick(n, 1024), 4)
    wk, wv = _layout_w_kv(w_kv_up[l])
    k, v = _kv_up(p, g_kv[l].reshape(1, -1), wk, wv, cos_t, sin_t, _pick(n, 1024), 4)
    o = _attention(q, k, v, batch, seq, _pick(seq, 256))
    c = _conv_module(p, _layout_w_dw(w_dw[l], d), b_dw[l].reshape(1, d), g_conv_ln[l].reshape(1, d),
                     b_conv_ln[l].reshape(1, d), batch, seq, d, _pick(seq, 256))
    merged = _merge(o, c, w_attn_o[l].astype(BF16), w_conv_out[l].astype(BF16), p, _pick(n, 1024), 512)
    wr = w_router[l]
    wr_hi = wr.astype(BF16)
    wr_lo = (wr - wr_hi.astype(F32)).astype(BF16)
    x2, t_rows, top_idx, gates = _res_router(merged, xf, w_out[l].astype(BF16), g_ffn[l].reshape(1, d),
                                             wr_hi, wr_lo, b_router[l].reshape(1, -1), _pick(n, 256))
    tok, dst, tile_expert, n_active, n_slots = _routing_plan(top_idx, tm_moe)
    hdn = _moe_up(tile_expert, n_active, _index_rows(tok, tm_moe), t_rows.reshape(n, SUBLANES, LANES),
                  w_gate_up[l].astype(BF16), b_gate_up[l].reshape(N_EXPERTS, 1, -1), n_slots, tm_moe, 512)
    y_rows = _moe_down(tile_expert, n_active, _index_rows(dst, tm_moe), hdn, w_down[l].astype(BF16),
                       b_down[l].reshape(N_EXPERTS, 1, -1), n * TOP_K + n_slots, tm_moe)
    out = _combine(y_rows, x2, gates, g_final.reshape(1, d), _pick(n, 256))
    return out.reshape(batch, seq, d)
```

```python
import functools

import jax
import jax.numpy as jnp
from jax import lax
from jax.experimental import pallas as pl
from jax.experimental.pallas import tpu as pltpu

N_HEADS = 16
QK_NOPE_DIM = 128
QK_ROPE_DIM = 64
V_HEAD_DIM = 128
Q_LORA_RANK = 768
KV_LORA_RANK = 512
ROPE_THETA = 10000.0
ATTN_SCALE = (QK_NOPE_DIM + QK_ROPE_DIM) ** -0.5
CONV_WIDTH = 31
N_EXPERTS = 32
TOP_K = 4
SWIGLU_LIMIT = 7.0
SWIGLU_ALPHA = 1.702
NORM_EPS = 1e-6

LANES = 128
SUBLANES = 8
HEAD_PAD = 2 * LANES
VMEM_LIMIT = 56 * 1024 * 1024

F32 = jnp.float32
BF16 = jnp.bfloat16
U32 = jnp.uint32


def _cparams(*sem):
    return pltpu.CompilerParams(dimension_semantics=sem, vmem_limit_bytes=VMEM_LIMIT)


def _rms(x, g):
    ms = jnp.mean(x * x, axis=-1, keepdims=True)
    return x * lax.rsqrt(ms + NORM_EPS) * g


def _rope_tables_kernel(pos_ref, invf_ref, cos_ref, sin_ref):
    ang = pos_ref[...].astype(F32) * invf_ref[...]
    lane = lax.broadcasted_iota(jnp.int32, ang.shape, 1)
    c = jnp.cos(ang)
    s = jnp.sin(ang)
    cos_ref[...] = jnp.where(lane < QK_ROPE_DIM, c, 0.0)
    sin_ref[...] = jnp.where(lane < QK_ROPE_DIM // 2, -s, jnp.where(lane < QK_ROPE_DIM, s, 0.0))


def _rope_tables(pos_b, invf, tm):
    n = pos_b.shape[0]
    row = pl.BlockSpec((tm, LANES), lambda i: (i, 0))
    return pl.pallas_call(
        _rope_tables_kernel, grid=(n // tm,),
        in_specs=[row, pl.BlockSpec((1, LANES), lambda i: (0, 0))],
        out_specs=[row, row],
        out_shape=[jax.ShapeDtypeStruct((n, LANES), F32)] * 2,
        compiler_params=_cparams("parallel"), name="rope_tables")(pos_b, invf)


def _rope(r, cos, sin):
    half = QK_ROPE_DIM // 2
    lane = lax.broadcasted_iota(jnp.int32, r.shape, 1)
    rot = jnp.where(lane < half, pltpu.roll(r, LANES - half, 1), pltpu.roll(r, half, 1))
    return r * cos + rot * sin


def _in_proj_kernel(x_ref, g_ref, w_ref, o_ref, h_ref):
    @pl.when(pl.program_id(1) == 0)
    def _():
        h_ref[...] = _rms(x_ref[...], g_ref[...]).astype(h_ref.dtype)

    o_ref[...] = jnp.dot(h_ref[...], w_ref[...], preferred_element_type=F32).astype(o_ref.dtype)


def _in_proj(x2d, g, w, tm, tn):
    n, d = x2d.shape
    cols = w.shape[1]
    return pl.pallas_call(
        _in_proj_kernel, grid=(n // tm, cols // tn),
        in_specs=[pl.BlockSpec((tm, d), lambda i, j: (i, 0)),
                  pl.BlockSpec((1, d), lambda i, j: (0, 0)),
                  pl.BlockSpec((d, tn), lambda i, j: (0, j))],
        out_specs=pl.BlockSpec((tm, tn), lambda i, j: (i, j)),
        out_shape=jax.ShapeDtypeStruct((n, cols), BF16),
        scratch_shapes=[pltpu.VMEM((tm, d), BF16)],
        compiler_params=_cparams("parallel", "arbitrary"), name="in_proj")(x2d, g, w)


def _q_up_kernel(c_ref, g_ref, w_ref, cos_ref, sin_ref, o_ref, cn_ref, *, hb):
    @pl.when(pl.program_id(1) == 0)
    def _():
        cn_ref[...] = _rms(c_ref[...].astype(F32), g_ref[...]).astype(cn_ref.dtype)

    q = jnp.dot(cn_ref[...], w_ref[...], preferred_element_type=F32)
    cos = cos_ref[...]
    sin = sin_ref[...]
    for h in range(hb):
        b0 = h * HEAD_PAD
        o_ref[:, b0:b0 + LANES] = (q[:, b0:b0 + LANES] * ATTN_SCALE).astype(o_ref.dtype)
        o_ref[:, b0 + LANES:b0 + HEAD_PAD] = (
            _rope(q[:, b0 + LANES:b0 + HEAD_PAD], cos, sin) * ATTN_SCALE).astype(o_ref.dtype)


def _q_up(p, g, w, cos_t, sin_t, tm, hb):
    n = p.shape[0]
    r = Q_LORA_RANK
    row = pl.BlockSpec((tm, LANES), lambda i, j: (i, 0))
    return pl.pallas_call(
        functools.partial(_q_up_kernel, hb=hb), grid=(n // tm, N_HEADS // hb),
        in_specs=[pl.BlockSpec((tm, r), lambda i, j: (i, 1)),
                  pl.BlockSpec((1, r), lambda i, j: (0, 0)),
                  pl.BlockSpec((r, hb * HEAD_PAD), lambda i, j: (0, j)),
                  row, row],
        out_specs=pl.BlockSpec((tm, hb * HEAD_PAD), lambda i, j: (i, j)),
        out_shape=jax.ShapeDtypeStruct((n, N_HEADS * HEAD_PAD), BF16),
        scratch_shapes=[pltpu.VMEM((tm, r), BF16)],
        compiler_params=_cparams("parallel", "arbitrary"), name="q_up")(p, g, w, cos_t, sin_t)


def _kv_up_kernel(c_ref, g_ref, wk_ref, wv_ref, cos_ref, sin_ref, k_ref, v_ref, cn_ref, kr_ref, *, hb):
    @pl.when(pl.program_id(1) == 0)
    def _():
        cn_ref[...] = _rms(c_ref[:, :KV_LORA_RANK].astype(F32), g_ref[...]).astype(cn_ref.dtype)
        kr = c_ref[:, KV_LORA_RANK:KV_LORA_RANK + LANES].astype(F32)
        kr_ref[...] = _rope(kr, cos_ref[...], sin_ref[...]).astype(kr_ref.dtype)

    cn = cn_ref[...]
    kn = jnp.dot(cn, wk_ref[...], preferred_element_type=F32)
    for h in range(hb):
        k_ref[:, h * HEAD_PAD:h * HEAD_PAD + LANES] = kn[:, h * LANES:(h + 1) * LANES].astype(k_ref.dtype)
        k_ref[:, h * HEAD_PAD + LANES:(h + 1) * HEAD_PAD] = kr_ref[...]
    v_ref[...] = jnp.dot(cn, wv_ref[...], preferred_element_type=F32).astype(v_ref.dtype)


def _kv_up(p, g, wk, wv, cos_t, sin_t, tm, hb):
    n = p.shape[0]
    r = KV_LORA_RANK
    cw = Q_LORA_RANK
    row = pl.BlockSpec((tm, LANES), lambda i, j: (i, 0))
    return pl.pallas_call(
        functools.partial(_kv_up_kernel, hb=hb), grid=(n // tm, N_HEADS // hb),
        in_specs=[pl.BlockSpec((tm, cw), lambda i, j: (i, 0)),
                  pl.BlockSpec((1, r), lambda i, j: (0, 0)),
                  pl.BlockSpec((r, hb * LANES), lambda i, j: (0, j)),
                  pl.BlockSpec((r, hb * LANES), lambda i, j: (0, j)),
                  row, row],
        out_specs=[pl.BlockSpec((tm, hb * HEAD_PAD), lambda i, j: (i, j)),
                   pl.BlockSpec((tm, hb * LANES), lambda i, j: (i, j))],
        out_shape=[jax.ShapeDtypeStruct((n, N_HEADS * HEAD_PAD), BF16),
                   jax.ShapeDtypeStruct((n, N_HEADS * V_HEAD_DIM), BF16)],
        scratch_shapes=[pltpu.VMEM((tm, r), BF16), pltpu.VMEM((tm, LANES), BF16)],
        compiler_params=_cparams("parallel", "arbitrary"), name="kv_up")(p, g, wk, wv, cos_t, sin_t)


def _attn_kernel(q_ref, k_ref, v_ref, o_ref, *, tq):
    s_len = q_ref.shape[0]
    dn = (((1,), (1,)), ((), ()))
    row = lax.broadcasted_iota(jnp.int32, (tq, tq), 0)
    col = lax.broadcasted_iota(jnp.int32, (tq, tq), 1)
    for i in range(s_len // tq):
        lo = i * tq
        q = q_ref[lo:lo + tq, :]
        sd = lax.dot_general(q, k_ref[lo:lo + tq, :], dn, preferred_element_type=F32)
        sd = jnp.where(col <= row, sd, -jnp.inf)
        m = jnp.max(sd, axis=-1, keepdims=True)
        if i > 0:
            sp = lax.dot_general(q, k_ref[0:lo, :], dn, preferred_element_type=F32)
            m = jnp.maximum(m, jnp.max(sp, axis=-1, keepdims=True))
            pp = jnp.exp(sp - m)
            l = jnp.sum(pp, axis=-1, keepdims=True)
            acc = jnp.dot(pp.astype(v_ref.dtype), v_ref[0:lo, :], preferred_element_type=F32)
        pd = jnp.exp(sd - m)
        ld = jnp.sum(pd, axis=-1, keepdims=True)
        accd = jnp.dot(pd.astype(v_ref.dtype), v_ref[lo:lo + tq, :], preferred_element_type=F32)
        if i > 0:
            ld = ld + l
            accd = accd + acc
        o_ref[lo:lo + tq, :] = (accd / ld).astype(o_ref.dtype)


def _attention(q, k, v, batch, seq, tq):
    n = q.shape[0]
    return pl.pallas_call(
        functools.partial(_attn_kernel, tq=tq), grid=(batch, N_HEADS),
        in_specs=[pl.BlockSpec((seq, HEAD_PAD), lambda b, h: (b, h)),
                  pl.BlockSpec((seq, HEAD_PAD), lambda b, h: (b, h)),
                  pl.BlockSpec((seq, V_HEAD_DIM), lambda b, h: (b, h))],
        out_specs=pl.BlockSpec((seq, V_HEAD_DIM), lambda b, h: (b, h)),
        out_shape=jax.ShapeDtypeStruct((n, N_HEADS * V_HEAD_DIM), BF16),
        compiler_params=_cparams("parallel", "parallel"), name="attention")(q, k, v)


CONV_HALO = 32
CONV_CW = 512
CONV_RC = 32


def _conv_kernel(a_ref, b_ref, w_ref, bdw_ref, g_ref, bln_ref, o_ref, gbuf, shbuf, cbuf, *, ts):
    si = pl.program_id(1)
    ch = a_ref.shape[1]
    ncc = ch // CONV_CW
    sh_rows = shbuf.shape[1]

    @pl.when(si == 0)
    def _():
        gbuf[:, 0:CONV_HALO, :] = jnp.zeros((ncc, CONV_HALO, CONV_CW), F32)

    @pl.when(si > 0)
    def _():
        gbuf[:, 0:CONV_HALO, :] = gbuf[:, ts:ts + CONV_HALO, :]

    for cc in range(ncc):
        a = a_ref[:, cc * CONV_CW:(cc + 1) * CONV_CW].astype(F32)
        b = b_ref[:, cc * CONV_CW:(cc + 1) * CONV_CW].astype(F32)
        gbuf[cc, CONV_HALO:CONV_HALO + ts, :] = a * jax.nn.sigmoid(b)

    def chunk(cc, carry):
        for j in range(1, SUBLANES):
            shbuf[j - 1] = gbuf[cc, j:j + sh_rows, :]
        for r in range(ts // CONV_RC):
            acc = jnp.zeros((CONV_RC, CONV_CW), F32)
            for kk in range(CONV_WIDTH):
                off = r * CONV_RC + CONV_HALO - (CONV_WIDTH - 1) + kk
                j = off % SUBLANES
                base = off - j
                rows = gbuf[cc, base:base + CONV_RC, :] if j == 0 else shbuf[j - 1, base:base + CONV_RC, :]
                acc = acc + w_ref[cc, kk:kk + 1, :] * rows
            cbuf[cc, r * CONV_RC:(r + 1) * CONV_RC, :] = acc
        return carry

    lax.fori_loop(0, ncc, chunk, 0)

    c = [cbuf[cc] + bdw_ref[:, cc * CONV_CW:(cc + 1) * CONV_CW] for cc in range(ncc)]
    mu = sum(jnp.sum(x, axis=-1, keepdims=True) for x in c) / ch
    var = sum(jnp.sum(jnp.square(x - mu), axis=-1, keepdims=True) for x in c) / ch
    rstd = lax.rsqrt(var + NORM_EPS)
    for cc in range(ncc):
        sl = slice(cc * CONV_CW, (cc + 1) * CONV_CW)
        y = (c[cc] - mu) * rstd * g_ref[:, sl] + bln_ref[:, sl]
        o_ref[:, sl] = (y * jax.nn.sigmoid(y)).astype(o_ref.dtype)


def _conv_module(p, w4, b_dw, g_ln, b_ln, batch, seq, ch, ts):
    n = p.shape[0]
    nseq = seq // ts
    ncc = ch // CONV_CW
    vec = pl.BlockSpec((1, ch), lambda b, s: (0, 0))
    return pl.pallas_call(
        functools.partial(_conv_kernel, ts=ts), grid=(batch, nseq),
        in_specs=[pl.BlockSpec((ts, ch), lambda b, s: (b * nseq + s, 1)),
                  pl.BlockSpec((ts, ch), lambda b, s: (b * nseq + s, 2)),
                  pl.BlockSpec((ncc, CONV_HALO, CONV_CW), lambda b, s: (0, 0, 0)),
                  vec, vec, vec],
        out_specs=pl.BlockSpec((ts, ch), lambda b, s: (b * nseq + s, 0)),
        out_shape=jax.ShapeDtypeStruct((n, ch), BF16),
        scratch_shapes=[pltpu.VMEM((ncc, CONV_HALO + ts, CONV_CW), F32),
                        pltpu.VMEM((SUBLANES - 1, CONV_HALO + ts - SUBLANES, CONV_CW), F32),
                        pltpu.VMEM((ncc, ts, CONV_CW), F32)],
        compiler_params=_cparams("parallel", "arbitrary"), name="conv_module")(p, p, w4, b_dw, g_ln, b_ln)


def _merge_kernel(o_ref, c_ref, wa_ref, wc_ref, ga_ref, gb_ref, out_ref):
    ya = jnp.dot(o_ref[...], wa_ref[...], preferred_element_type=F32)
    yb = jnp.dot(c_ref[...], wc_ref[...], preferred_element_type=F32)
    out = jax.nn.sigmoid(ga_ref[...].astype(F32)) * ya + jax.nn.sigmoid(gb_ref[...].astype(F32)) * yb
    out_ref[...] = out.astype(out_ref.dtype)


def _merge(o, c, wa, wc, p, tm, tn):
    n, d = o.shape
    nj = d // tn
    return pl.pallas_call(
        _merge_kernel, grid=(n // tm, nj),
        in_specs=[pl.BlockSpec((tm, d), lambda i, j: (i, 0)),
                  pl.BlockSpec((tm, d), lambda i, j: (i, 0)),
                  pl.BlockSpec((d, tn), lambda i, j: (0, j)),
                  pl.BlockSpec((d, tn), lambda i, j: (0, j)),
                  pl.BlockSpec((tm, tn), lambda i, j: (i, 3 * nj + j)),
                  pl.BlockSpec((tm, tn), lambda i, j: (i, 4 * nj + j))],
        out_specs=pl.BlockSpec((tm, tn), lambda i, j: (i, j)),
        out_shape=jax.ShapeDtypeStruct((n, d), BF16),
        compiler_params=_cparams("parallel", "arbitrary"), name="merge")(o, c, wa, wc, p, p)


HI_MASK = 0xFFFF0000


def _pack_rows(ref, lead, val, rows):
    half = val.shape[1] // 2
    ns = half // LANES
    lo = lax.bitcast_convert_type(val[:, :half].astype(BF16).astype(F32), U32) >> 16
    hi = lax.bitcast_convert_type(val[:, half:].astype(BF16).astype(F32), U32) & U32(HI_MASK)
    w = lo | hi
    for s in range(ns):
        ref[lead + (pl.ds(s, rows, stride=ns), slice(None))] = w[:, s * LANES:(s + 1) * LANES]


def _unpack_slab(words):
    lo = lax.bitcast_convert_type(words << 16, F32)
    hi = lax.bitcast_convert_type(words & U32(HI_MASK), F32)
    return lo, hi


def _res_router_kernel(m_ref, x_ref, w_ref, g_ref, wrh_ref, wrl_ref, br_ref,
                       x2_ref, t_ref, idx_ref, gate_ref, *, tm):
    x2 = x_ref[...] + jnp.dot(m_ref[...], w_ref[...], preferred_element_type=F32)
    x2_ref[...] = x2
    t = _rms(x2, g_ref[...])
    _pack_rows(t_ref, (), t, tm)
    t_hi = t.astype(BF16)
    t_lo = (t - t_hi.astype(F32)).astype(BF16)
    logits = (jnp.dot(t_hi, wrh_ref[...], preferred_element_type=F32)
              + jnp.dot(t_lo, wrh_ref[...], preferred_element_type=F32)
              + jnp.dot(t_hi, wrl_ref[...], preferred_element_type=F32)) + br_ref[...]
    ne = logits.shape[1]
    lane = lax.broadcasted_iota(jnp.int32, logits.shape, 1)
    k_lane = lax.broadcasted_iota(jnp.int32, (tm, TOP_K), 1)
    idx_out = jnp.zeros((tm, TOP_K), jnp.int32)
    val_out = jnp.zeros((tm, TOP_K), F32)
    cur = logits
    for kk in range(TOP_K):
        mx = jnp.max(cur, axis=-1, keepdims=True)
        am = jnp.min(jnp.where(cur == mx, lane, ne), axis=-1, keepdims=True)
        idx_out = jnp.where(k_lane == kk, am, idx_out)
        val_out = jnp.where(k_lane == kk, mx, val_out)
        cur = jnp.where(lane == am, -jnp.inf, cur)
    e = jnp.exp(val_out - val_out[:, 0:1])
    gate_ref[...] = e / jnp.sum(e, axis=-1, keepdims=True)
    idx_ref[...] = idx_out


def _res_router(merged, x2d, w_out, g_ffn, wr_hi, wr_lo, b_r, tm):
    n, d = x2d.shape
    ne = wr_hi.shape[1]
    ns = d // (2 * LANES)
    rowd = pl.BlockSpec((tm, d), lambda i: (i, 0))
    rowk = pl.BlockSpec((tm, TOP_K), lambda i: (i, 0))
    const = lambda shape: pl.BlockSpec(shape, lambda i: (0, 0))
    return pl.pallas_call(
        functools.partial(_res_router_kernel, tm=tm), grid=(n // tm,),
        in_specs=[rowd, rowd, const((d, d)), const((1, d)), const((d, ne)), const((d, ne)), const((1, ne))],
        out_specs=[rowd, pl.BlockSpec((tm * ns, LANES), lambda i: (i, 0)), rowk, rowk],
        out_shape=[jax.ShapeDtypeStruct((n, d), F32),
                   jax.ShapeDtypeStruct((n * ns, LANES), U32),
                   jax.ShapeDtypeStruct((n, TOP_K), jnp.int32),
                   jax.ShapeDtypeStruct((n, TOP_K), F32)],
        compiler_params=_cparams("parallel"), name="res_router")(merged, x2d, w_out, g_ffn, wr_hi, wr_lo, b_r)


def _idx_copy(idx_hbm, ibuf, isem, tile, slot):
    return pltpu.make_async_copy(idx_hbm.at[pl.ds(pl.multiple_of(tile * SUBLANES, SUBLANES), SUBLANES)],
                                 ibuf.at[slot], isem.at[slot])


def _wait_all_rows(buf, sem, slot):
    pltpu.make_async_copy(buf.at[slot], buf.at[slot], sem.at[slot]).wait()


def _gather_rows(i, n, idx_hbm, src_hbm, ibuf, isem, buf, sem, rows, ns):
    def issue_rows(slot):
        def body(r, carry):
            pltpu.make_async_copy(src_hbm.at[ibuf[slot, 0, r]],
                                  buf.at[slot, pl.ds(pl.multiple_of(r * ns, ns), ns)],
                                  sem.at[slot]).start()
            return carry
        lax.fori_loop(0, rows, body, 0)

    @pl.when(i == 0)
    def _():
        _idx_copy(idx_hbm, ibuf, isem, 0, 0).start()
        _idx_copy(idx_hbm, ibuf, isem, 0, 0).wait()
        issue_rows(0)

        @pl.when(n > 1)
        def _():
            _idx_copy(idx_hbm, ibuf, isem, 1, 1).start()

    @pl.when(i + 1 < n)
    def _():
        nslot = (i + 1) % 2
        _idx_copy(idx_hbm, ibuf, isem, i + 1, nslot).wait()
        issue_rows(nslot)

    @pl.when(i + 2 < n)
    def _():
        _idx_copy(idx_hbm, ibuf, isem, i + 2, i % 2).start()

    slot = i % 2
    _wait_all_rows(buf, sem, slot)
    return slot


def _moe_up_kernel(te_ref, na_ref, idx_hbm, t_hbm, w_ref, b_ref, o_ref, ibuf, isem, buf, sem, x_ref, *, tm, tf):
    i = pl.program_id(0)
    n_act = na_ref[0]
    f = o_ref.shape[1]
    half = x_ref.shape[1] // 2

    @pl.when(i < n_act)
    def _():
        slot = _gather_rows(i, n_act, idx_hbm, t_hbm, ibuf, isem, buf, sem, tm, SUBLANES)
        for s in range(SUBLANES):
            lo, hi = _unpack_slab(buf[slot, pl.ds(s, tm, stride=SUBLANES), :])
            x_ref[:, s * LANES:(s + 1) * LANES] = lo.astype(x_ref.dtype)
            x_ref[:, half + s * LANES:half + (s + 1) * LANES] = hi.astype(x_ref.dtype)
        x = x_ref[...]
        for c in range(f // tf):
            gs = slice(c * tf, (c + 1) * tf)
            us = slice(f + c * tf, f + (c + 1) * tf)
            g = jnp.dot(x, w_ref[0, :, gs], preferred_element_type=F32) + b_ref[0, :, gs]
            u = jnp.dot(x, w_ref[0, :, us], preferred_element_type=F32) + b_ref[0, :, us]
            g = jnp.minimum(g, SWIGLU_LIMIT)
            u = jnp.clip(u, -SWIGLU_LIMIT, SWIGLU_LIMIT)
            o_ref[:, gs] = ((u + 1.0) * (g * jax.nn.sigmoid(SWIGLU_ALPHA * g))).astype(o_ref.dtype)

    @pl.when(i >= n_act)
    def _():
        o_ref[...] = jnp.zeros(o_ref.shape, o_ref.dtype)


def _moe_up(tile_expert, n_active, tok8, t3, w_gu, b_gu, n_slots, tm, tf):
    d = w_gu.shape[1]
    f = w_gu.shape[2] // 2
    grid_spec = pltpu.PrefetchScalarGridSpec(
        num_scalar_prefetch=2, grid=(n_slots // tm,),
        in_specs=[pl.BlockSpec(memory_space=pl.ANY), pl.BlockSpec(memory_space=pl.ANY),
                  pl.BlockSpec((1, d, 2 * f), lambda i, te, na: (te[i], 0, 0)),
                  pl.BlockSpec((1, 1, 2 * f), lambda i, te, na: (te[i], 0, 0))],
        out_specs=pl.BlockSpec((tm, f), lambda i, te, na: (i, 0)),
        scratch_shapes=[pltpu.SMEM((2, SUBLANES, tm), jnp.int32), pltpu.SemaphoreType.DMA((2,)),
                        pltpu.VMEM((2, tm * SUBLANES, LANES), U32), pltpu.SemaphoreType.DMA((2,)),
                        pltpu.VMEM((tm, d), BF16)])
    return pl.pallas_call(
        functools.partial(_moe_up_kernel, tm=tm, tf=tf), grid_spec=grid_spec,
        out_shape=jax.ShapeDtypeStruct((n_slots, f), BF16),
        compiler_params=_cparams("arbitrary"), name="moe_up")(tile_expert, n_active, tok8, t3, w_gu, b_gu)


def _moe_down_kernel(te_ref, na_ref, dst_hbm, y_init, h_ref, w_ref, b_ref, y_hbm, ibuf, isem, stage, sem, *, tm):
    del y_init
    i = pl.program_id(0)
    n_act = na_ref[0]

    @pl.when(i < n_act)
    def _():
        slot = i % 2

        @pl.when(i == 0)
        def _():
            _idx_copy(dst_hbm, ibuf, isem, 0, 0).start()

        @pl.when(i + 1 < n_act)
        def _():
            _idx_copy(dst_hbm, ibuf, isem, i + 1, 1 - slot).start()

        @pl.when(i >= 2)
        def _():
            _wait_all_rows(stage, sem, slot)

        y = jnp.dot(h_ref[...], w_ref[0], preferred_element_type=F32) + b_ref[0]
        _pack_rows(stage, (slot,), y, tm)
        _idx_copy(dst_hbm, ibuf, isem, i, slot).wait()

        def body(r, carry):
            pltpu.make_async_copy(stage.at[slot, pl.ds(pl.multiple_of(r * SUBLANES, SUBLANES), SUBLANES)],
                                  y_hbm.at[ibuf[slot, 0, r]], sem.at[slot]).start()
            return carry
        lax.fori_loop(0, tm, body, 0)

        @pl.when(i == n_act - 1)
        def _():
            _wait_all_rows(stage, sem, slot)

            @pl.when(i >= 1)
            def _():
                _wait_all_rows(stage, sem, 1 - slot)


def _moe_down(tile_expert, n_active, dst8, hdn, w_d, b_d, n_rows, tm):
    n_slots, f = hdn.shape
    d = w_d.shape[2]
    grid_spec = pltpu.PrefetchScalarGridSpec(
        num_scalar_prefetch=2, grid=(n_slots // tm,),
        in_specs=[pl.BlockSpec(memory_space=pl.ANY), pl.BlockSpec(memory_space=pl.ANY),
                  pl.BlockSpec((tm, f), lambda i, te, na: (i, 0)),
                  pl.BlockSpec((1, f, d), lambda i, te, na: (te[i], 0, 0)),
                  pl.BlockSpec((1, 1, d), lambda i, te, na: (te[i], 0, 0))],
        out_specs=pl.BlockSpec(memory_space=pl.ANY),
        scratch_shapes=[pltpu.SMEM((2, SUBLANES, tm), jnp.int32), pltpu.SemaphoreType.DMA((2,)),
                        pltpu.VMEM((2, tm * SUBLANES, LANES), U32), pltpu.SemaphoreType.DMA((2,))])
    y_init = jnp.zeros((n_rows, SUBLANES, LANES), U32)
    return pl.pallas_call(
        functools.partial(_moe_down_kernel, tm=tm), grid_spec=grid_spec,
        out_shape=jax.ShapeDtypeStruct((n_rows, SUBLANES, LANES), U32),
        input_output_aliases={3: 0},
        compiler_params=_cparams("arbitrary"), name="moe_down")(tile_expert, n_active, dst8, y_init, hdn, w_d, b_d)


def _combine_kernel(y0_ref, y1_ref, y2_ref, y3_ref, x2_ref, gate_ref, g_ref, o_ref, m_ref, *, tc):
    half = m_ref.shape[1] // 2
    gate = gate_ref[...]
    gk = [jnp.broadcast_to(gate[:, kk:kk + 1], (tc, LANES)) for kk in range(TOP_K)]
    for s in range(SUBLANES):
        lo_acc = hi_acc = None
        for kk, y_ref in enumerate((y0_ref, y1_ref, y2_ref, y3_ref)):
            lo, hi = _unpack_slab(y_ref[pl.ds(s, tc, stride=SUBLANES), :])
            lo_acc = gk[kk] * lo if lo_acc is None else lo_acc + gk[kk] * lo
            hi_acc = gk[kk] * hi if hi_acc is None else hi_acc + gk[kk] * hi
        m_ref[:, s * LANES:(s + 1) * LANES] = lo_acc
        m_ref[:, half + s * LANES:half + (s + 1) * LANES] = hi_acc
    o_ref[...] = _rms(x2_ref[...] + m_ref[...], g_ref[...])


def _combine(y_rows, x2, gates, g_final, tc):
    n, d = x2.shape
    nt = n // tc
    y2d = y_rows.reshape(y_rows.shape[0] * SUBLANES, LANES)
    y_specs = [pl.BlockSpec((tc * SUBLANES, LANES), lambda i, kk=kk: (kk * nt + i, 0)) for kk in range(TOP_K)]
    return pl.pallas_call(
        functools.partial(_combine_kernel, tc=tc), grid=(nt,),
        in_specs=y_specs + [pl.BlockSpec((tc, d), lambda i: (i, 0)),
                            pl.BlockSpec((tc, TOP_K), lambda i: (i, 0)),
                            pl.BlockSpec((1, d), lambda i: (0, 0))],
        out_specs=pl.BlockSpec((tc, d), lambda i: (i, 0)),
        out_shape=jax.ShapeDtypeStruct((n, d), F32),
        scratch_shapes=[pltpu.VMEM((tc, d), F32)],
        compiler_params=_cparams("parallel"), name="combine")(y2d, y2d, y2d, y2d, x2, gates, g_final)


def _index_rows(idx, rows):
    t = idx.shape[0] // rows
    return jnp.pad(idx.reshape(t, 1, rows), ((0, 0), (0, SUBLANES - 1), (0, 0))).reshape(t * SUBLANES, rows)


def _routing_plan(top_idx, tm):
    n, k = top_idx.shape
    a = n * k
    e_flat = top_idx.reshape(a)
    onehot = (e_flat[:, None] == jnp.arange(N_EXPERTS, dtype=jnp.int32)[None, :]).astype(jnp.int32)
    csum = jnp.cumsum(onehot, axis=0)
    rank = jnp.sum(onehot * csum, axis=1) - 1
    counts = csum[-1]
    tiles_e = (counts + tm - 1) // tm
    tile_end = jnp.cumsum(tiles_e)
    group_start = (tile_end - tiles_e) * tm
    slot = (group_start[e_flat] + rank).astype(jnp.int32)
    n_slots = a + N_EXPERTS * tm
    n_tiles = n_slots // tm
    slot_a = jnp.full((n_slots,), -1, jnp.int32).at[slot].set(jnp.arange(a, dtype=jnp.int32))
    valid = slot_a >= 0
    tok = jnp.where(valid, slot_a // k, 0)
    dst = jnp.where(valid, (slot_a % k) * n + slot_a // k, a + jnp.arange(n_slots, dtype=jnp.int32))
    n_active = tile_end[-1].astype(jnp.int32)
    tile_id = jnp.minimum(jnp.arange(n_tiles, dtype=jnp.int32), n_active - 1)
    tile_expert = jnp.minimum(jnp.searchsorted(tile_end, tile_id, side="right"), N_EXPERTS - 1).astype(jnp.int32)
    return tok, dst, tile_expert, n_active.reshape(1), n_slots


def _layout_w_in(w, d):
    o_q = Q_LORA_RANK
    o_kv = o_q + KV_LORA_RANK
    o_kr = o_kv + QK_ROPE_DIM
    z = lambda c: jnp.zeros((w.shape[0], c), w.dtype)
    pad_kr = Q_LORA_RANK - KV_LORA_RANK - QK_ROPE_DIM
    mla = jnp.concatenate([w[:, o_q:o_kv], w[:, o_kv:o_kr], z(pad_kr), w[:, :o_q], z(d - 2 * Q_LORA_RANK)], axis=1)
    return jnp.concatenate([mla, w[:, o_kr:]], axis=1).astype(BF16)


def _layout_w_q(w):
    r = w.shape[0]
    w3 = w.reshape(r, N_HEADS, QK_NOPE_DIM + QK_ROPE_DIM)
    w3 = jnp.pad(w3, ((0, 0), (0, 0), (0, HEAD_PAD - QK_NOPE_DIM - QK_ROPE_DIM)))
    return w3.reshape(r, N_HEADS * HEAD_PAD).astype(BF16)


def _layout_w_kv(w):
    r = w.shape[0]
    w3 = w.reshape(r, N_HEADS, QK_NOPE_DIM + V_HEAD_DIM)
    wk = w3[:, :, :QK_NOPE_DIM].reshape(r, N_HEADS * QK_NOPE_DIM)
    wv = w3[:, :, QK_NOPE_DIM:].reshape(r, N_HEADS * V_HEAD_DIM)
    return wk.astype(BF16), wv.astype(BF16)


def _layout_w_dw(w, ch):
    ncc = ch // CONV_CW
    w = jnp.pad(w, ((0, CONV_HALO - CONV_WIDTH), (0, 0)))
    return w.reshape(CONV_HALO, ncc, CONV_CW).transpose(1, 0, 2)


def _pick(n, pref):
    t = min(n, pref)
    while n % t:
        t //= 2
    return t


def kernel(x, positions, g_mix, w_in, g_q, w_q_up, g_kv, w_kv_up, w_attn_o, w_dw, b_dw, g_conv_ln, b_conv_ln,
           w_conv_out, w_out, g_ffn, w_router, b_router, w_gate_up, b_gate_up, w_down, b_down, g_final):
    batch, seq, d = x.shape
    n = batch * seq
    assert g_mix.shape[0] == 1, "one decoder layer"
    assert d == 2 * SUBLANES * LANES, "a packed token row is one (8, 128) tile"
    l = 0
    xf = x.reshape(n, d)

    half = QK_ROPE_DIM // 2
    inv_freq = ROPE_THETA ** (-jnp.arange(0, QK_ROPE_DIM, 2, dtype=F32) / QK_ROPE_DIM)
    invf = jnp.concatenate([inv_freq, inv_freq, jnp.zeros((LANES - 2 * half,), F32)]).reshape(1, LANES)
    pos_b = jnp.broadcast_to(positions.reshape(n, 1), (n, LANES))
    cos_t, sin_t = _rope_tables(pos_b, invf, _pick(n, 1024))

    tm_moe = _pick(n * TOP_K, 512)
    p = _in_proj(xf, g_mix[l].reshape(1, d), _layout_w_in(w_in[l], d), _pick(n, 1024), 512)
    q = _q_up(p, g_q[l].reshape(1, -1), _layout_w_q(w_q_up[l]), cos_t, sin_t, _pick(n, 1024), 4)
    wk, wv = _layout_w_kv(w_kv_up[l])
    k, v = _kv_up(p, g_kv[l].reshape(1, -1), wk, wv, cos_t, sin_t, _pick(n, 1024), 4)
    o = _attention(q, k, v, batch, seq, _pick(seq, 256))
    c = _conv_module(p, _layout_w_dw(w_dw[l], d), b_dw[l].reshape(1, d), g_conv_ln[l].reshape(1, d),
                     b_conv_ln[l].reshape(1, d), batch, seq, d, _pick(seq, 256))
    merged = _merge(o, c, w_attn_o[l].astype(BF16), w_conv_out[l].astype(BF16), p, _pick(n, 1024), 512)
    wr = w_router[l]
    wr_hi = wr.astype(BF16)
    wr_lo = (wr - wr_hi.astype(F32)).astype(BF16)
    x2, t_rows, top_idx, gates = _res_router(merged, xf, w_out[l].astype(BF16), g_ffn[l].reshape(1, d),
                                             wr_hi, wr_lo, b_router[l].reshape(1, -1), _pick(n, 256))
    tok, dst, tile_expert, n_active, n_slots = _routing_plan(top_idx, tm_moe)
    hdn = _moe_up(tile_expert, n_active, _index_rows(tok, tm_moe), t_rows.reshape(n, SUBLANES, LANES),
                  w_gate_up[l].astype(BF16), b_gate_up[l].reshape(N_EXPERTS, 1, -1), n_slots, tm_moe, 512)
    y_rows = _moe_down(tile_expert, n_active, _index_rows(dst, tm_moe), hdn, w_down[l].astype(BF16),
                       b_down[l].reshape(N_EXPERTS, 1, -1), n * TOP_K + n_slots, tm_moe)
    out = _combine(y_rows, x2, gates, g_final.reshape(1, d), _pick(n, 256))
    return out.reshape(batch, seq, d)
```
